```python
import math
import jax
import jax.numpy as jnp
from jax import lax
import numpy as np

D_MODEL = 2048
BATCH = 8
SEQ = 4096
DEPTH = 4

CTX_LEN = 256
GRID_W = 64
N_DIR = 2
D_MIX = D_MODEL
SSD_WIDTH = D_MIX // 2
GDN_WIDTH = D_MIX - SSD_WIDTH
SSD_HEAD_DIM = 64
SSD_HEADS = SSD_WIDTH // SSD_HEAD_DIM
SSD_GROUPS = 2
SSD_HEADS_PER_GROUP = SSD_HEADS // SSD_GROUPS
SSD_STATE = 128
SSD_CONV_DIM = SSD_WIDTH + 2 * SSD_GROUPS * SSD_STATE
GDN_HEAD_DIM = 128
GDN_HEADS = GDN_WIDTH // GDN_HEAD_DIM
GDN_CONV_DIM = 3 * GDN_WIDTH
CONV_K = 5
CHUNK = 128
EPS = 1e-6

OFF_XBC = D_MIX
OFF_DT = OFF_XBC + SSD_CONV_DIM
OFF_QKV = OFF_DT + N_DIR * SSD_HEADS
OFF_BETA = OFF_QKV + GDN_CONV_DIM
OFF_ALPHA = OFF_BETA + N_DIR * GDN_HEADS
IN_DIM = OFF_ALPHA + N_DIR * GDN_HEADS

kernel_name = 'hybrid_ssd_gdn_prefix_dit'


def rms_norm(x, w):
    xf = x.astype(jnp.float32)
    y = xf * lax.rsqrt(jnp.mean(xf * xf, axis=-1, keepdims=True) + EPS)
    return (y * w.astype(jnp.float32)).astype(x.dtype)


def l2_norm(t):
    return t * lax.rsqrt(jnp.sum(t * t, axis=-1, keepdims=True) + EPS)


def _flip(t):
    return t[:, ::-1]


def _same(t):
    return t


def centred_dwconv(x, w):
    pad = CONV_K // 2
    L = x.shape[1]
    xp = jnp.pad(x, ((0, 0), (pad, pad), (0, 0)))
    out = xp[:, 0:L] * w[0]
    for k in range(1, CONV_K):
        out = out + xp[:, k:k + L] * w[k]
    return out


def conv_latent(x, w):
    bsz, L, ch = x.shape
    rows = L // GRID_W
    y = centred_dwconv(x.reshape(bsz * rows, GRID_W, ch), w)
    return y.reshape(bsz, L, ch)


def _chunks(t):
    return t.reshape(t.shape[0], t.shape[1] // CHUNK, CHUNK, *t.shape[2:])


def ssd_scan(xs, dt, a_neg, bm, cm, h0):
    loga = _chunks(dt * a_neg)
    xdt = _chunks(xs * dt[..., None])
    bmc, cmc = _chunks(bm), _chunks(cm)
    acum = jnp.cumsum(loga, axis=2)
    causal = jnp.tril(jnp.ones((CHUNK, CHUNK), bool))
    a_t = jnp.moveaxis(acum, 2, -1)
    seg = a_t[..., :, None] - a_t[..., None, :]
    lmat = jnp.exp(jnp.where(causal, seg, -jnp.inf))
    cb = jnp.einsum('bcqgn,bcsgn->bcgqs', cmc, bmc)
    y_diag = jnp.einsum('bcgrqs,bcsgrp->bcqgrp', cb[:, :, :, None] * lmat, xdt)
    decay_end = jnp.exp(acum[:, :, -1:] - acum)
    states = jnp.einsum('bcsgn,bcsgrp->bcgrpn', bmc, xdt * decay_end[..., None])
    chunk_decay = jnp.exp(acum[:, :, -1])

    def step(h, inp):
        st, dec = inp
        return h * dec[..., None, None] + st, h

    h_last, h_prev = lax.scan(step, h0, (jnp.moveaxis(states, 1, 0), jnp.moveaxis(chunk_decay, 1, 0)))
    h_prev = jnp.moveaxis(h_prev, 0, 1)
    y_off = jnp.einsum('bcqgn,bcgrpn->bcqgrp', cmc, h_prev) * jnp.exp(acum)[..., None]
    return (y_diag + y_off).reshape(xs.shape), h_last


def gdn_scan(q, k, v, g, beta, s0):
    bsz, L, H, V = v.shape

    def heads_first(t):
        return jnp.moveaxis(_chunks(t), 3, 1)

    q, k, v, g, beta = (heads_first(t) for t in (q, k, v, g, beta))
    gcum = jnp.cumsum(g, axis=-1)
    incl = jnp.tril(jnp.ones((CHUNK, CHUNK), bool))
    strict = jnp.tril(jnp.ones((CHUNK, CHUNK), bool), -1)
    decay = jnp.exp(jnp.where(incl, gcum[..., :, None] - gcum[..., None, :], -jnp.inf))
    k_beta = k * beta[..., None]
    a_strict = jnp.where(strict, jnp.einsum('bhcik,bhcjk->bhcij', k_beta, k) * decay, 0.0)
    rhs = jnp.concatenate([v * beta[..., None], k_beta * jnp.exp(gcum)[..., None]], axis=-1)
    sol = lax.linalg.triangular_solve(a_strict, rhs, left_side=True, lower=True, unit_diagonal=True)
    u, w = sol[..., :V], sol[..., V:]
    attn = jnp.einsum('bhcik,bhcjk->bhcij', q, k) * decay
    q_dec = q * jnp.exp(gcum)[..., None]
    k_end = k * jnp.exp(gcum[..., -1:] - gcum)[..., None]
    chunk_decay = jnp.exp(gcum[..., -1])

    def step(S, inp):
        qd, ke, ui, wi, ai, dec = inp
        v_new = ui - jnp.einsum('bhqk,bhkv->bhqv', wi, S)
        o = jnp.einsum('bhqk,bhkv->bhqv', qd, S) + jnp.einsum('bhqs,bhsv->bhqv', ai, v_new)
        S = S * dec[..., None, None] + jnp.einsum('bhsk,bhsv->bhkv', ke, v_new)
        return S, o

    xs = tuple(jnp.moveaxis(t, 2, 0) for t in (q_dec, k_end, u, w, attn, chunk_decay))
    s_last, o = lax.scan(step, s0, xs)
    o = jnp.moveaxis(jnp.moveaxis(o, 0, 2), 1, 3).reshape(bsz, L, H, V)
    return o, s_last


def ssd_mixer(xbc_c, dt_c, xbc_l, dt_l, a_log, dt_bias, d_skip):
    G, R, P, N = SSD_GROUPS, SSD_HEADS_PER_GROUP, SSD_HEAD_DIM, SSD_STATE
    a_neg = -jnp.exp(a_log.astype(jnp.float32)).reshape(N_DIR, G, R)
    bias = dt_bias.astype(jnp.float32).reshape(N_DIR, G, R)

    def unpack(xbc, dt_raw):
        bsz, L = xbc.shape[:2]
        xbc = xbc.astype(jnp.float32)
        xs = xbc[..., :SSD_WIDTH].reshape(bsz, L, G, R, P)
        bm = xbc[..., SSD_WIDTH:SSD_WIDTH + G * N].reshape(bsz, L, G, N)
        cm = xbc[..., SSD_WIDTH + G * N:].reshape(bsz, L, G, N)
        dt = jax.nn.softplus(dt_raw.astype(jnp.float32).reshape(bsz, L, N_DIR, G, R) + bias)
        return xs, bm, cm, dt

    xc, bc, cc, dtc = unpack(xbc_c, dt_c)
    xl, bl, cl, dtl = unpack(xbc_l, dt_l)
    dsk = d_skip.astype(jnp.float32).reshape(G, R)[..., None]
    y_c, y_l = xc * dsk, xl * dsk
    h0 = jnp.zeros((xc.shape[0], G, R, P, N), jnp.float32)
    for d in range(N_DIR):
        f = _flip if d else _same
        yc_d, h_ctx = ssd_scan(f(xc), f(dtc[:, :, d]), a_neg[d], f(bc), f(cc), h0)
        yl_d, _ = ssd_scan(f(xl), f(dtl[:, :, d]), a_neg[d], f(bl), f(cl), h_ctx)
        y_c = y_c + f(yc_d)
        y_l = y_l + f(yl_d)
    return y_c.reshape(*xc.shape[:2], SSD_WIDTH), y_l.reshape(*xl.shape[:2], SSD_WIDTH)


def gdn_mixer(qkv_c, b_c, a_c, qkv_l, b_l, a_l, a_log, dt_bias):
    H, K = GDN_HEADS, GDN_HEAD_DIM
    rate = jnp.exp(a_log.astype(jnp.float32))
    bias = dt_bias.astype(jnp.float32)

    def unpack(qkv, b_raw, a_raw):
        bsz, L = qkv.shape[:2]
        qkv = qkv.astype(jnp.float32).reshape(bsz, L, 3, H, K)
        q = l2_norm(qkv[:, :, 0]) * (K ** -0.5)
        k = l2_norm(qkv[:, :, 1])
        v = qkv[:, :, 2]
        beta = jax.nn.sigmoid(b_raw.astype(jnp.float32).reshape(bsz, L, N_DIR, H))
        g = -rate * jax.nn.softplus(a_raw.astype(jnp.float32).reshape(bsz, L, N_DIR, H) + bias)
        return q, k, v, beta, g

    qc, kc, vc, bec, gc = unpack(qkv_c, b_c, a_c)
    ql, kl, vl, bel, gl = unpack(qkv_l, b_l, a_l)
    s0 = jnp.zeros((qc.shape[0], H, K, K), jnp.float32)
    o_c = jnp.zeros_like(vc)
    o_l = jnp.zeros_like(vl)
    for d in range(N_DIR):
        f = _flip if d else _same
        oc_d, s_ctx = gdn_scan(f(qc), f(kc), f(vc), f(gc[:, :, d]), f(bec[:, :, d]), s0)
        ol_d, _ = gdn_scan(f(ql), f(kl), f(vl), f(gl[:, :, d]), f(bel[:, :, d]), s_ctx)
        o_c = o_c + f(oc_d)
        o_l = o_l + f(ol_d)
    return o_c, o_l


def ssd_out_norm(y, z, w):
    bsz, L = y.shape[:2]
    yg = (y * jax.nn.silu(z.astype(jnp.float32))).reshape(bsz, L, SSD_GROUPS, SSD_WIDTH // SSD_GROUPS)
    yg = yg * lax.rsqrt(jnp.mean(yg * yg, axis=-1, keepdims=True) + EPS)
    return yg.reshape(bsz, L, SSD_WIDTH) * w.astype(jnp.float32)


def gdn_out_norm(o, z, w):
    bsz, L = o.shape[:2]
    o = o * lax.rsqrt(jnp.mean(o * o, axis=-1, keepdims=True) + EPS) * w.astype(jnp.float32)
    return o.reshape(bsz, L, GDN_WIDTH) * jax.nn.silu(z.astype(jnp.float32))


def _split_proj(p):
    return (p[..., :OFF_XBC], p[..., OFF_XBC:OFF_DT], p[..., OFF_DT:OFF_QKV],
            p[..., OFF_QKV:OFF_BETA], p[..., OFF_BETA:OFF_ALPHA], p[..., OFF_ALPHA:IN_DIM])


def hybrid_layer(h_lat, h_ctx, mod_lat, mod_ctx, pre_w, post_w, w_in, conv_ssd_w, conv_ssd_b,
                 conv_gdn_w, ssd_a_log, ssd_dt_bias, ssd_d, ssd_norm_w, gdn_a_log, gdn_dt_bias,
                 gdn_norm_w, w_out, update_ctx):
    shift_l, scale_l, gate_l = jnp.split(mod_lat, 3, axis=-1)
    shift_c, scale_c, gate_c = jnp.split(mod_ctx, 3, axis=-1)
    u_l = rms_norm(h_lat, pre_w) * (1 + scale_l[:, None]) + shift_l[:, None]
    u_c = rms_norm(h_ctx, pre_w) * (1 + scale_c) + shift_c
    z_l, xbc_l, dt_l, qkv_l, b_l, a_l = _split_proj(u_l @ w_in)
    z_c, xbc_c, dt_c, qkv_c, b_c, a_c = _split_proj(u_c @ w_in)
    xbc_l = jax.nn.silu(conv_latent(xbc_l, conv_ssd_w) + conv_ssd_b)
    xbc_c = jax.nn.silu(centred_dwconv(xbc_c, conv_ssd_w) + conv_ssd_b)
    qkv_l = jax.nn.silu(conv_latent(qkv_l, conv_gdn_w))
    qkv_c = jax.nn.silu(centred_dwconv(qkv_c, conv_gdn_w))
    ssd_c, ssd_l = ssd_mixer(xbc_c, dt_c, xbc_l, dt_l, ssd_a_log, ssd_dt_bias, ssd_d)
    gdn_c, gdn_l = gdn_mixer(qkv_c, b_c, a_c, qkv_l, b_l, a_l, gdn_a_log, gdn_dt_bias)

    def merge(ssd_y, gdn_o, z, dtype):
        y = jnp.concatenate([ssd_out_norm(ssd_y, z[..., :SSD_WIDTH], ssd_norm_w),
                             gdn_out_norm(gdn_o, z[..., SSD_WIDTH:], gdn_norm_w)], axis=-1)
        return rms_norm(y.astype(dtype) @ w_out, post_w)

    h_lat = h_lat + gate_l[:, None] * merge(ssd_l, gdn_l, z_l, h_lat.dtype)
    if update_ctx:
        h_ctx = h_ctx + gate_c * merge(ssd_c, gdn_c, z_c, h_ctx.dtype)
    return h_lat, h_ctx


def _dt_bias_init(key, shape):
    dt = jnp.exp(jax.random.uniform(key, shape, jnp.float32, math.log(1e-3), math.log(1e-1)))
    return dt + jnp.log(-jnp.expm1(-dt))


def setup_inputs(seed: int = 0) -> dict:
    key = jax.random.key(seed)
    ks = jax.random.split(key, 20)
    f32 = jnp.float32

    def nrm(k, shape, s):
        return jax.random.normal(k, shape, f32) * s

    return {
        'x': nrm(ks[0], (BATCH, SEQ, D_MODEL), 1.0),
        'c': nrm(ks[1], (BATCH, D_MODEL), 1.0),
        'ctx': nrm(ks[2], (BATCH, CTX_LEN, D_MODEL), 1.0),
        'c_ctx': nrm(ks[3], (D_MODEL,), 1.0),
        'w_ada': nrm(ks[4], (DEPTH, D_MODEL, 3 * D_MODEL), 0.5 * D_MODEL ** -0.5),
        'b_ada': nrm(ks[5], (DEPTH, 3 * D_MODEL), 0.01),
        'pre_norm_w': 1.0 + nrm(ks[6], (DEPTH, D_MODEL), 0.05),
        'post_norm_w': 1.0 + nrm(ks[7], (DEPTH, D_MODEL), 0.05),
        'w_in': nrm(ks[8], (DEPTH, D_MODEL, IN_DIM), D_MODEL ** -0.5),
        'conv_ssd_w': nrm(ks[9], (DEPTH, CONV_K, SSD_CONV_DIM), CONV_K ** -0.5),
        'conv_ssd_b': nrm(ks[10], (DEPTH, SSD_CONV_DIM), 0.02),
        'conv_gdn_w': nrm(ks[11], (DEPTH, CONV_K, GDN_CONV_DIM), CONV_K ** -0.5),
        'ssd_a_log': jnp.log(jax.random.uniform(ks[12], (DEPTH, N_DIR, SSD_HEADS), f32, 1.0, 16.0)),
        'ssd_dt_bias': _dt_bias_init(ks[13], (DEPTH, N_DIR, SSD_HEADS)),
        'ssd_d': 1.0 + nrm(ks[14], (DEPTH, SSD_HEADS), 0.1),
        'ssd_norm_w': 1.0 + nrm(ks[15], (DEPTH, SSD_WIDTH), 0.05),
        'gdn_a_log': jnp.log(jax.random.uniform(ks[16], (DEPTH, N_DIR, GDN_HEADS), f32, 1.0, 16.0)),
        'gdn_dt_bias': _dt_bias_init(ks[17], (DEPTH, N_DIR, GDN_HEADS)),
        'gdn_norm_w': 1.0 + nrm(ks[18], (DEPTH, GDN_HEAD_DIM), 0.05),
        'w_out': nrm(ks[19], (DEPTH, D_MIX, D_MODEL), D_MIX ** -0.5),
    }


def reference(x, c, ctx, c_ctx, w_ada, b_ada, pre_norm_w, post_norm_w, w_in, conv_ssd_w, conv_ssd_b,
              conv_gdn_w, ssd_a_log, ssd_dt_bias, ssd_d, ssd_norm_w, gdn_a_log, gdn_dt_bias,
              gdn_norm_w, w_out):
    h_lat, h_ctx = x, ctx
    c_act = jax.nn.silu(c)
    cc_act = jax.nn.silu(c_ctx)
    for l in range(DEPTH):
        mod_lat = c_act @ w_ada[l] + b_ada[l]
        mod_ctx = cc_act @ w_ada[l] + b_ada[l]
        h_lat, h_ctx = hybrid_layer(
            h_lat, h_ctx, mod_lat, mod_ctx, pre_norm_w[l], post_norm_w[l], w_in[l],
            conv_ssd_w[l], conv_ssd_b[l], conv_gdn_w[l], ssd_a_log[l], ssd_dt_bias[l], ssd_d[l],
            ssd_norm_w[l], gdn_a_log[l], gdn_dt_bias[l], gdn_norm_w[l], w_out[l],
            update_ctx=(l < DEPTH - 1))
    return h_lat
```

```python
import functools

import jax
import jax.numpy as jnp
from jax import lax
from jax.experimental import pallas as pl
from jax.experimental.pallas import tpu as pltpu

F32 = jnp.float32
BF16 = jnp.bfloat16
HIGHEST = lax.Precision.HIGHEST

EPS = 1e-6
CHUNK = 128
GRID_W = 64
CONV_K = 5
N_DIR = 2
SSD_HEAD_DIM = 64
SSD_STATE = 128
SSD_GROUPS = 2
GDN_HEAD_DIM = 128
LANE = 128
VMEM_LIMIT = 52 * 1024 * 1024

NT = (((1,), (1,)), ((), ()))
TN = (((0,), (0,)), ((), ()))


def _dot(a, b):
    return jnp.dot(a, b, preferred_element_type=F32)


def _dot_nt(a, b):
    return lax.dot_general(a, b, NT, preferred_element_type=F32)


def _dot_tn(a, b):
    return lax.dot_general(a, b, TN, preferred_element_type=F32)


def _sigmoid(x):
    return 1.0 / (1.0 + jnp.exp(-x))


def _silu(x):
    return x * _sigmoid(x)


def _softplus(x):
    return jnp.maximum(x, 0.0) + jnp.log(1.0 + jnp.exp(-jnp.abs(x)))


def _params(*sem):
    return pltpu.CompilerParams(dimension_semantics=sem, vmem_limit_bytes=VMEM_LIMIT)


def _mod_kernel(c_ref, w_ref, b_ref, o_ref):
    a = _silu(c_ref[...])
    o_ref[0] = jnp.dot(a, w_ref[0], precision=HIGHEST, preferred_element_type=F32) + b_ref[0]


def _modulation(c_all, w_ada, b_ada):
    depth, d, n = w_ada.shape
    rows = c_all.shape[0]
    tn = 768
    return pl.pallas_call(
        _mod_kernel,
        grid=(depth, n // tn),
        in_specs=[pl.BlockSpec((rows, d), lambda l, j: (0, 0)),
                  pl.BlockSpec((1, d, tn), lambda l, j: (l, 0, j)),
                  pl.BlockSpec((1, 1, tn), lambda l, j: (l, 0, j))],
        out_specs=pl.BlockSpec((1, rows, tn), lambda l, j: (l, 0, j)),
        out_shape=jax.ShapeDtypeStruct((depth, rows, n), F32),
        compiler_params=_params("parallel", "parallel"),
    )(c_all, w_ada, b_ada.reshape(depth, 1, n))


def _inproj_kernel(h_ref, shift_ref, scale_ref, prew_ref, w_ref, cw_ref, cb_ref, o_ref, u_ref,
                   *, group, j_lo, j_hi):
    j = pl.program_id(1)
    tm, tn = o_ref.shape

    @pl.when(j == 0)
    def _():
        x = h_ref[...]
        y = x * lax.rsqrt(jnp.mean(x * x, axis=-1, keepdims=True) + EPS) * prew_ref[...]
        u_ref[...] = (y * (1.0 + scale_ref[0, 0]) + shift_ref[0, 0]).astype(BF16)

    o_ref[...] = _dot(u_ref[...], w_ref[...])

    @pl.when((j >= j_lo) & (j < j_hi))
    def _():
        pos = lax.broadcasted_iota(jnp.int32, (group, tn), 0)
        cw = cw_ref[...]
        cb = cb_ref[...]

        def strip(s, carry):
            r0 = pl.multiple_of(s * group, group)
            a = o_ref[pl.ds(r0, group), :]
            out = a * cw[2:3]
            for k in (0, 1, 3, 4):
                off = k - CONV_K // 2
                shifted = pltpu.roll(a, (-off) % group, axis=0)
                valid = (pos >= -off) if off < 0 else (pos < group - off)
                out = out + jnp.where(valid, shifted, 0.0) * cw[k:k + 1]
            o_ref[pl.ds(r0, group), :] = _silu(out + cb)
            return carry

        lax.fori_loop(0, tm // group, strip, 0)


def _inproj(h2d, shift, scale, pre_w, w, cw, cb, *, rows_per_mod, group, tm, tn, conv_cols):
    m, d = h2d.shape
    wp = w.shape[1]
    tiles_per_mod = rows_per_mod // tm
    kern = functools.partial(_inproj_kernel, group=group, j_lo=conv_cols[0] // tn, j_hi=conv_cols[1] // tn)
    return pl.pallas_call(
        kern,
        grid=(m // tm, wp // tn),
        in_specs=[pl.BlockSpec((tm, d), lambda i, j: (i, 0)),
                  pl.BlockSpec((1, 1, 1, d), lambda i, j: (i // tiles_per_mod, 0, 0, 0)),
                  pl.BlockSpec((1, 1, 1, d), lambda i, j: (i // tiles_per_mod, 0, 0, 1)),
                  pl.BlockSpec((1, d), lambda i, j: (0, 0)),
                  pl.BlockSpec((d, tn), lambda i, j: (0, j)),
                  pl.BlockSpec((8, tn), lambda i, j: (0, j)),
                  pl.BlockSpec((1, tn), lambda i, j: (0, j))],
        out_specs=pl.BlockSpec((tm, tn), lambda i, j: (i, j)),
        out_shape=jax.ShapeDtypeStruct((m, wp), F32),
        scratch_shapes=[pltpu.VMEM((tm, d), BF16)],
        compiler_params=_params("parallel", "arbitrary"),
    )(h2d, shift, scale, pre_w, w, cw, cb)


def _direction_masks(d):
    ri = lax.broadcasted_iota(jnp.int32, (CHUNK, CHUNK), 0)
    ci = lax.broadcasted_iota(jnp.int32, (CHUNK, CHUNK), 1)
    diff = (ri - ci) * (1 - 2 * d)
    return diff >= 0, diff > 0, ri, ci


def _chunk_index(d, p, nc):
    return p + d * (nc - 1 - 2 * p)


def _ssd_kernel(xs_ref, bc_ref, sm_ref, par_ref, dsk_ref, h0_ref, y_ref, hT_ref):
    d = pl.program_id(1)
    p = pl.program_id(2)
    heads_per_group = xs_ref.shape[-1] // (SSD_GROUPS * SSD_HEAD_DIM)
    gw = heads_per_group * SSD_HEAD_DIM

    @pl.when(p == 0)
    def _():
        hT_ref[...] = h0_ref[...]

    incl, _, _, _ = _direction_masks(d)
    fwd = d == 0
    sm = sm_ref[0]
    par = par_ref[0]
    dt = _softplus(sm + par[0:1])
    loga = dt * par[1:2]
    acum = jnp.dot(incl.astype(F32), loga, precision=HIGHEST, preferred_element_type=F32)
    acum_t = acum.T
    dt_t = dt.T
    alast = jnp.where(fwd, acum[CHUNK - 1:CHUNK], acum[0:1])
    eacum = jnp.exp(acum)
    wdec = dt * jnp.exp(alast - acum)
    cdec = jnp.exp(alast)
    skip = dsk_ref[...] * (1 - d).astype(F32)

    xs = xs_ref[0]
    bc = bc_ref[0]
    for g in range(SSD_GROUPS):
        bg = bc[:, g * SSD_STATE:(g + 1) * SSD_STATE].astype(BF16)
        cg = bc[:, (SSD_GROUPS + g) * SSD_STATE:(SSD_GROUPS + g + 1) * SSD_STATE].astype(BF16)
        cb = _dot_nt(cg, bg)
        h_t = hT_ref[0, 0, g]
        yoff = _dot(cg, h_t.astype(BF16))
        ys, xw, cds = [], [], []
        for r in range(heads_per_group):
            hh = g * heads_per_group + r
            seg = acum[:, hh:hh + 1] - acum_t[hh:hh + 1, :]
            lmat = jnp.where(incl, jnp.exp(seg), 0.0)
            mm = (cb * lmat * dt_t[hh:hh + 1, :]).astype(BF16)
            xh = xs[:, hh * SSD_HEAD_DIM:(hh + 1) * SSD_HEAD_DIM]
            yd = _dot(mm, xh.astype(BF16))
            ys.append(yd + yoff[:, r * SSD_HEAD_DIM:(r + 1) * SSD_HEAD_DIM] * eacum[:, hh:hh + 1])
            xw.append(xh * wdec[:, hh:hh + 1])
            cds.append(jnp.broadcast_to(cdec[:, hh:hh + 1], (1, SSD_HEAD_DIM)))
        upd = _dot_tn(bg, jnp.concatenate(xw, axis=1).astype(BF16))
        hT_ref[0, 0, g] = h_t * jnp.concatenate(cds, axis=1) + upd
        xg = xs[:, g * gw:(g + 1) * gw]
        y_ref[0, 0, :, g * gw:(g + 1) * gw] = jnp.concatenate(ys, axis=1) + xg * skip[:, g * gw:(g + 1) * gw]


def _ssd_scan(p3d, par, dsk, h0, *, cols):
    bsz, length, _ = p3d.shape
    nc = length // CHUNK
    c_xs, c_bc, c_sm, w_xs, w_bc = cols
    chunk = functools.partial(_chunk_index, nc=nc)
    state_shape = h0.shape
    return pl.pallas_call(
        _ssd_kernel,
        grid=(bsz, N_DIR, nc),
        in_specs=[pl.BlockSpec((1, CHUNK, w_xs), lambda b, d, p: (b, chunk(d, p), c_xs // w_xs)),
                  pl.BlockSpec((1, CHUNK, w_bc), lambda b, d, p: (b, chunk(d, p), c_bc // w_bc)),
                  pl.BlockSpec((1, CHUNK, LANE), lambda b, d, p: (b, chunk(d, p), c_sm // LANE + d)),
                  pl.BlockSpec((1, 8, LANE), lambda b, d, p: (d, 0, 0)),
                  pl.BlockSpec((1, w_xs), lambda b, d, p: (0, 0)),
                  pl.BlockSpec((1, 1) + state_shape[2:], lambda b, d, p: (b, d, 0, 0, 0))],
        out_specs=[pl.BlockSpec((1, 1, CHUNK, w_xs), lambda b, d, p: (d, b, chunk(d, p), 0)),
                   pl.BlockSpec((1, 1) + state_shape[2:], lambda b, d, p: (b, d, 0, 0, 0))],
        out_shape=[jax.ShapeDtypeStruct((N_DIR, bsz, length, w_xs), F32),
                   jax.ShapeDtypeStruct(state_shape, F32)],
        compiler_params=_params("parallel", "parallel", "arbitrary"),
    )(p3d, p3d, p3d, par, dsk, h0)


def _unit_triangular_inverse(a, xor_ij):
    ri = lax.broadcasted_iota(jnp.int32, a.shape, 0)
    ci = lax.broadcasted_iota(jnp.int32, a.shape, 1)
    t = jnp.where(ri == ci, 1.0, 0.0) - jnp.where(xor_ij < 2, a, 0.0)
    size = 2
    while size < CHUNK:
        shift = size.bit_length() - 1
        a_off = jnp.where((xor_ij >> shift) == 1, a, 0.0).astype(BF16)
        t16 = t.astype(BF16)
        t = t - _dot(_dot(t16, a_off).astype(BF16), t16)
        size *= 2
    return t


def _gdn_kernel(q_ref, k_ref, v_ref, sm_ref, par_ref, s0_ref, o_ref, s_ref, *, beta_lane, g_lane):
    d = pl.program_id(1)
    p = pl.program_id(2)
    heads = q_ref.shape[-1] // GDN_HEAD_DIM

    @pl.when(p == 0)
    def _():
        s_ref[...] = s0_ref[...]

    incl, strict, ri, ci = _direction_masks(d)
    xor_ij = ri ^ ci
    fwd = d == 0
    sm = sm_ref[0]
    par = par_ref[0]
    beta_all = _sigmoid(sm)
    g_all = par[1:2] * _softplus(sm + par[0:1])
    gcum = jnp.dot(incl.astype(F32), g_all, precision=HIGHEST, preferred_element_type=F32)
    gcum_t = gcum.T
    glast = jnp.where(fwd, gcum[CHUNK - 1:CHUNK], gcum[0:1])
    eg = jnp.exp(gcum)
    eend = jnp.exp(glast - gcum)
    cdec = jnp.exp(glast)

    for h in range(heads):
        lo, hi = h * GDN_HEAD_DIM, (h + 1) * GDN_HEAD_DIM
        q = q_ref[0, :, lo:hi]
        k = k_ref[0, :, lo:hi]
        v = v_ref[0, :, lo:hi]
        qn = q * (lax.rsqrt(jnp.sum(q * q, axis=-1, keepdims=True) + EPS) * GDN_HEAD_DIM ** -0.5)
        kn = k * lax.rsqrt(jnp.sum(k * k, axis=-1, keepdims=True) + EPS)
        beta = beta_all[:, beta_lane + h:beta_lane + h + 1]
        gl = g_lane + h
        dec = jnp.where(incl, jnp.exp(gcum[:, gl:gl + 1] - gcum_t[gl:gl + 1, :]), 0.0)
        kb = kn * beta
        k16 = kn.astype(BF16)
        a = jnp.where(strict, _dot_nt(kb.astype(BF16), k16) * dec, 0.0)
        t = _unit_triangular_inverse(a, xor_ij)
        rhs = jnp.concatenate([v * beta, kb * eg[:, gl:gl + 1]], axis=1).astype(BF16)
        sol = _dot(t.astype(BF16), rhs)
        u = sol[:, :GDN_HEAD_DIM]
        w = sol[:, GDN_HEAD_DIM:]
        attn = (_dot_nt(qn.astype(BF16), k16) * dec).astype(BF16)
        s = s_ref[0, 0, h]
        s16 = s.astype(BF16)
        qd = (qn * eg[:, gl:gl + 1]).astype(BF16)
        ws = _dot(jnp.concatenate([w.astype(BF16), qd], axis=0), s16)
        vnew = (u - ws[:CHUNK]).astype(BF16)
        o_ref[0, 0, :, lo:hi] = ws[CHUNK:] + _dot(attn, vnew)
        ke = (kn * eend[:, gl:gl + 1]).astype(BF16)
        s_ref[0, 0, h] = s * cdec[:, gl:gl + 1] + _dot_tn(ke, vnew)


def _gdn_scan(p3d, par, s0, *, cols, beta_lane, g_lane):
    bsz, length, _ = p3d.shape
    nc = length // CHUNK
    c_q, c_sm, w_q = cols
    chunk = functools.partial(_chunk_index, nc=nc)
    state_shape = s0.shape
    kern = functools.partial(_gdn_kernel, beta_lane=beta_lane, g_lane=g_lane)
    qkv_spec = [pl.BlockSpec((1, CHUNK, w_q), functools.partial(
        lambda b, d, p, i: (b, chunk(d, p), c_q // w_q + i), i=i)) for i in range(3)]
    return pl.pallas_call(
        kern,
        grid=(bsz, N_DIR, nc),
        in_specs=qkv_spec + [
            pl.BlockSpec((1, CHUNK, LANE), lambda b, d, p: (b, chunk(d, p), c_sm // LANE + d)),
            pl.BlockSpec((1, 8, LANE), lambda b, d, p: (d, 0, 0)),
            pl.BlockSpec((1, 1) + state_shape[2:], lambda b, d, p: (b, d, 0, 0, 0))],
        out_specs=[pl.BlockSpec((1, 1, CHUNK, w_q), lambda b, d, p: (d, b, chunk(d, p), 0)),
                   pl.BlockSpec((1, 1) + state_shape[2:], lambda b, d, p: (b, d, 0, 0, 0))],
        out_shape=[jax.ShapeDtypeStruct((N_DIR, bsz, length, w_q), F32),
                   jax.ShapeDtypeStruct(state_shape, F32)],
        compiler_params=_params("parallel", "parallel", "arbitrary"),
    )(p3d, p3d, p3d, p3d, par, s0)


def _outproj_kernel(y0_ref, y1_ref, o0_ref, o1_ref, z_ref, h_ref, gate_ref, w_ref,
                    snw_ref, gnw_ref, postw_ref, out_ref):
    zs = _silu(z_ref[...])
    ssd_w = y0_ref.shape[-1]
    gw = ssd_w // SSD_GROUPS
    ys = (y0_ref[0] + y1_ref[0]) * zs[:, :ssd_w]
    parts = []
    for g in range(SSD_GROUPS):
        yg = ys[:, g * gw:(g + 1) * gw]
        parts.append(yg * lax.rsqrt(jnp.mean(yg * yg, axis=-1, keepdims=True) + EPS))
    ysn = jnp.concatenate(parts, axis=1) * snw_ref[...]
    og = o0_ref[0] + o1_ref[0]
    parts = []
    for h in range(og.shape[-1] // GDN_HEAD_DIM):
        oh = og[:, h * GDN_HEAD_DIM:(h + 1) * GDN_HEAD_DIM]
        parts.append(oh * lax.rsqrt(jnp.mean(oh * oh, axis=-1, keepdims=True) + EPS))
    ogn = jnp.concatenate(parts, axis=1) * gnw_ref[...] * zs[:, ssd_w:]
    ycat = jnp.concatenate([ysn, ogn], axis=1).astype(BF16)
    m = _dot(ycat, w_ref[...])
    mn = m * lax.rsqrt(jnp.mean(m * m, axis=-1, keepdims=True) + EPS) * postw_ref[...]
    out_ref[...] = h_ref[...] + gate_ref[0, 0] * mn


def _outproj(y, o, p2d, h2d, gate, w, snw, gnw, postw, *, rows_per_mod, tm):
    m, d = h2d.shape
    ssd_w = y.shape[-1]
    gdn_w = o.shape[-1]
    tiles_per_mod = rows_per_mod // tm
    y2 = y.reshape(N_DIR, m, ssd_w)
    o2 = o.reshape(N_DIR, m, gdn_w)
    return pl.pallas_call(
        _outproj_kernel,
        grid=(m // tm,),
        in_specs=[pl.BlockSpec((1, tm, ssd_w), lambda i: (0, i, 0)),
                  pl.BlockSpec((1, tm, ssd_w), lambda i: (1, i, 0)),
                  pl.BlockSpec((1, tm, gdn_w), lambda i: (0, i, 0)),
                  pl.BlockSpec((1, tm, gdn_w), lambda i: (1, i, 0)),
                  pl.BlockSpec((tm, ssd_w + gdn_w), lambda i: (i, 0)),
                  pl.BlockSpec((tm, d), lambda i: (i, 0)),
                  pl.BlockSpec((1, 1, 1, d), lambda i: (i // tiles_per_mod, 0, 0, 2)),
                  pl.BlockSpec((ssd_w + gdn_w, d), lambda i: (0, 0)),
                  pl.BlockSpec((1, ssd_w), lambda i: (0, 0)),
                  pl.BlockSpec((1, gdn_w), lambda i: (0, 0)),
                  pl.BlockSpec((1, d), lambda i: (0, 0))],
        out_specs=pl.BlockSpec((tm, d), lambda i: (i, 0)),
        out_shape=jax.ShapeDtypeStruct((m, d), F32),
        compiler_params=_params("parallel"),
    )(y2, y2, o2, o2, p2d, h2d, gate, w, snw, gnw, postw)


def kernel(x, c, ctx, c_ctx, w_ada, b_ada, pre_norm_w, post_norm_w, w_in, conv_ssd_w, conv_ssd_b,
           conv_gdn_w, ssd_a_log, ssd_dt_bias, ssd_d, ssd_norm_w, gdn_a_log, gdn_dt_bias,
           gdn_norm_w, w_out):
    bsz, seq, d = x.shape
    ctx_len = ctx.shape[1]
    depth = w_in.shape[0]
    d_mix = w_out.shape[1]
    ssd_w = d_mix // 2
    gdn_w = d_mix - ssd_w
    ssd_heads = ssd_w // SSD_HEAD_DIM
    gdn_heads = gdn_w // GDN_HEAD_DIM
    bc_w = 2 * SSD_GROUPS * SSD_STATE
    assert ssd_w == gdn_w and ssd_heads + 2 * gdn_heads <= LANE
    assert seq % CHUNK == 0 and ctx_len % CHUNK == 0 and CHUNK % GRID_W == 0

    c_z, c_xs = 0, d_mix
    c_qkv = c_xs + ssd_w
    c_bc = c_qkv + 3 * gdn_w
    c_sm = c_bc + bc_w
    tn = 512
    wp = -(-(c_sm + N_DIR * LANE) // tn) * tn
    assert c_qkv % (3 * gdn_w) == 0 or c_qkv % gdn_w == 0
    assert c_xs % ssd_w == 0 and c_bc % bc_w == 0 and c_xs % tn == 0 and c_sm % tn == 0
    beta_lane, g_lane = ssd_heads, ssd_heads + gdn_heads

    o_xbc = d_mix
    o_dt = o_xbc + ssd_w + bc_w
    o_qkv = o_dt + N_DIR * ssd_heads
    o_beta = o_qkv + 3 * gdn_w
    o_alpha = o_beta + N_DIR * gdn_heads

    def small_cols(dd):
        parts = [w_in[:, :, o_dt + dd * ssd_heads:o_dt + (dd + 1) * ssd_heads],
                 w_in[:, :, o_beta + dd * gdn_heads:o_beta + (dd + 1) * gdn_heads],
                 w_in[:, :, o_alpha + dd * gdn_heads:o_alpha + (dd + 1) * gdn_heads]]
        used = ssd_heads + 2 * gdn_heads
        return parts + [jnp.zeros((depth, d, LANE - used), w_in.dtype)]

    w_cat = jnp.concatenate(
        [w_in[:, :, :d_mix], w_in[:, :, o_xbc:o_xbc + ssd_w], w_in[:, :, o_qkv:o_qkv + 3 * gdn_w],
         w_in[:, :, o_xbc + ssd_w:o_xbc + ssd_w + bc_w]] + small_cols(0) + small_cols(1)
        + [jnp.zeros((depth, d, wp - c_sm - N_DIR * LANE), w_in.dtype)], axis=-1).astype(BF16)

    def conv_layout(ssd_part, gdn_part):
        lead = ssd_part.shape[:-1]
        return jnp.concatenate(
            [jnp.zeros(lead + (d_mix,), F32), ssd_part[..., :ssd_w], gdn_part, ssd_part[..., ssd_w:],
             jnp.zeros(lead + (wp - c_sm,), F32)], axis=-1)

    cw_all = conv_layout(conv_ssd_w.astype(F32), conv_gdn_w.astype(F32))
    cw_all = jnp.concatenate([cw_all, jnp.zeros((depth, 8 - CONV_K, wp), F32)], axis=1)
    cb_all = conv_layout(conv_ssd_b.astype(F32)[:, None], jnp.zeros((depth, 1, 3 * gdn_w), F32))

    def lane_row(vals, lane0):
        return jnp.pad(vals, ((0, 0), (0, 0), (lane0, LANE - lane0 - vals.shape[-1])))

    zeros_rows = jnp.zeros((depth, N_DIR, 6, LANE), F32)
    ssd_par = jnp.concatenate([lane_row(ssd_dt_bias.astype(F32), 0)[:, :, None],
                               lane_row(-jnp.exp(ssd_a_log.astype(F32)), 0)[:, :, None], zeros_rows], axis=2)
    gdn_par = jnp.concatenate([lane_row(gdn_dt_bias.astype(F32), g_lane)[:, :, None],
                               lane_row(-jnp.exp(gdn_a_log.astype(F32)), g_lane)[:, :, None], zeros_rows], axis=2)
    dsk_all = jnp.repeat(ssd_d.astype(F32), SSD_HEAD_DIM, axis=-1)[:, None]
    gnw_all = jnp.tile(gdn_norm_w.astype(F32), (1, gdn_heads))[:, None]
    w_out16 = w_out.astype(BF16)

    rows = 16
    c_all = jnp.concatenate([c, c_ctx[None], jnp.zeros((rows - bsz - 1, d), c.dtype)], axis=0)
    mod = _modulation(c_all, w_ada, b_ada).reshape(depth, rows, 1, 3 * d)

    h_lat = x.reshape(bsz * seq, d)
    h_ctx = ctx.reshape(bsz * ctx_len, d)
    tm_lat = min(1024, seq)
    tm_ctx = min(1024, bsz * ctx_len)
    tm_out = 256
    conv_cols = (c_xs, c_sm)
    ssd_cols = (c_xs, c_bc, c_sm, ssd_w, bc_w)
    gdn_cols = (c_qkv, c_sm, gdn_w)
    ssd_state0 = jnp.zeros((bsz, N_DIR, SSD_GROUPS, SSD_STATE, ssd_w // SSD_GROUPS), F32)
    gdn_state0 = jnp.zeros((bsz, N_DIR, gdn_heads, GDN_HEAD_DIM, GDN_HEAD_DIM), F32)

    for l in range(depth):
        mod_lat = mod[l, :bsz, None]
        mod_ctx = mod[l, bsz:bsz + 1, None]
        pre_w = pre_norm_w[l][None]
        p_lat = _inproj(h_lat, mod_lat, mod_lat, pre_w, w_cat[l], cw_all[l], cb_all[l],
                        rows_per_mod=seq, group=GRID_W, tm=tm_lat, tn=tn, conv_cols=conv_cols)
        p_ctx = _inproj(h_ctx, mod_ctx, mod_ctx, pre_w, w_cat[l], cw_all[l], cb_all[l],
                        rows_per_mod=bsz * ctx_len, group=ctx_len, tm=tm_ctx, tn=tn, conv_cols=conv_cols)
        p_lat3 = p_lat.reshape(bsz, seq, wp)
        p_ctx3 = p_ctx.reshape(bsz, ctx_len, wp)

        y_ctx, ssd_state = _ssd_scan(p_ctx3, ssd_par[l], dsk_all[l], ssd_state0, cols=ssd_cols)
        y_lat, _ = _ssd_scan(p_lat3, ssd_par[l], dsk_all[l], ssd_state, cols=ssd_cols)
        o_ctx, gdn_state = _gdn_scan(p_ctx3, gdn_par[l], gdn_state0, cols=gdn_cols,
                                     beta_lane=beta_lane, g_lane=g_lane)
        o_lat, _ = _gdn_scan(p_lat3, gdn_par[l], gdn_state, cols=gdn_cols,
                             beta_lane=beta_lane, g_lane=g_lane)

        snw = ssd_norm_w[l][None].astype(F32)
        post_w = post_norm_w[l][None]
        h_lat_new = _outproj(y_lat, o_lat, p_lat, h_lat, mod_lat, w_out16[l], snw, gnw_all[l], post_w,
                             rows_per_mod=seq, tm=min(tm_out, seq))
        if l < depth - 1:
            h_ctx = _outproj(y_ctx, o_ctx, p_ctx, h_ctx, mod_ctx, w_out16[l], snw, gnw_all[l], post_w,
                             rows_per_mod=bsz * ctx_len, tm=min(tm_out, bsz * ctx_len))
        h_lat = h_lat_new
    return h_lat.reshape(bsz, seq, d)
```

```python
import functools

import jax
import jax.numpy as jnp
from jax import lax
from jax.experimental import pallas as pl
from jax.experimental.pallas import tpu as pltpu

F32 = jnp.float32
BF16 = jnp.bfloat16
HIGHEST = lax.Precision.HIGHEST

EPS = 1e-6
CHUNK = 128
GRID_W = 64
CONV_K = 5
N_DIR = 2
SSD_HEAD_DIM = 64
SSD_STATE = 128
SSD_GROUPS = 2
GDN_HEAD_DIM = 128
LANE = 128
VMEM_LIMIT = 52 * 1024 * 1024

NT = (((1,), (1,)), ((), ()))
TN = (((0,), (0,)), ((), ()))


def _dot(a, b):
    return jnp.dot(a, b, preferred_element_type=F32)


def _dot_nt(a, b):
    return lax.dot_general(a, b, NT, preferred_element_type=F32)


def _dot_tn(a, b):
    return lax.dot_general(a, b, TN, preferred_element_type=F32)


def _sigmoid(x):
    return 1.0 / (1.0 + jnp.exp(-x))


def _silu(x):
    return x * _sigmoid(x)


def _softplus(x):
    return jnp.maximum(x, 0.0) + jnp.log(1.0 + jnp.exp(-jnp.abs(x)))


def _params(*sem):
    return pltpu.CompilerParams(dimension_semantics=sem, vmem_limit_bytes=VMEM_LIMIT)


def _mod_kernel(c_ref, w_ref, b_ref, o_ref):
    a = _silu(c_ref[...])
    o_ref[0] = jnp.dot(a, w_ref[0], precision=HIGHEST, preferred_element_type=F32) + b_ref[0]


def _modulation(c_all, w_ada, b_ada):
    depth, d, n = w_ada.shape
    rows = c_all.shape[0]
    tn = 768
    return pl.pallas_call(
        _mod_kernel,
        name="adaln_mod",
        grid=(depth, n // tn),
        in_specs=[pl.BlockSpec((rows, d), lambda l, j: (0, 0)),
                  pl.BlockSpec((1, d, tn), lambda l, j: (l, 0, j)),
                  pl.BlockSpec((1, 1, tn), lambda l, j: (l, 0, j))],
        out_specs=pl.BlockSpec((1, rows, tn), lambda l, j: (l, 0, j)),
        out_shape=jax.ShapeDtypeStruct((depth, rows, n), F32),
        compiler_params=_params("parallel", "parallel"),
    )(c_all, w_ada, b_ada.reshape(depth, 1, n))


def _inproj_kernel(h_ref, shift_ref, scale_ref, prew_ref, w_ref, cw_ref, cb_ref, o_ref, u_ref,
                   *, group, j_lo, j_hi):
    j = pl.program_id(1)
    tm, tn = o_ref.shape

    @pl.when(j == 0)
    def _():
        x = h_ref[...]
        y = x * lax.rsqrt(jnp.mean(x * x, axis=-1, keepdims=True) + EPS) * prew_ref[...]
        u_ref[...] = (y * (1.0 + scale_ref[0, 0]) + shift_ref[0, 0]).astype(BF16)

    o_ref[...] = _dot(u_ref[...], w_ref[...])

    @pl.when((j >= j_lo) & (j < j_hi))
    def _():
        pos = lax.broadcasted_iota(jnp.int32, (group, tn), 0)
        cw = cw_ref[...]
        cb = cb_ref[...]

        def strip(s, carry):
            r0 = pl.multiple_of(s * group, group)
            a = o_ref[pl.ds(r0, group), :]
            out = a * cw[2:3]
            for k in (0, 1, 3, 4):
                off = k - CONV_K // 2
                shifted = pltpu.roll(a, (-off) % group, axis=0)
                valid = (pos >= -off) if off < 0 else (pos < group - off)
                out = out + jnp.where(valid, shifted, 0.0) * cw[k:k + 1]
            o_ref[pl.ds(r0, group), :] = _silu(out + cb)
            return carry

        lax.fori_loop(0, tm // group, strip, 0)


def _inproj(h2d, shift, scale, pre_w, w, cw, cb, *, rows_per_mod, group, tm, tn, conv_cols):
    m, d = h2d.shape
    wp = w.shape[1]
    tiles_per_mod = rows_per_mod // tm
    kern = functools.partial(_inproj_kernel, group=group, j_lo=conv_cols[0] // tn, j_hi=conv_cols[1] // tn)
    return pl.pallas_call(
        kern,
        name="inproj",
        grid=(m // tm, wp // tn),
        in_specs=[pl.BlockSpec((tm, d), lambda i, j: (i, 0)),
                  pl.BlockSpec((1, 1, 1, d), lambda i, j: (i // tiles_per_mod, 0, 0, 0)),
                  pl.BlockSpec((1, 1, 1, d), lambda i, j: (i // tiles_per_mod, 0, 0, 1)),
                  pl.BlockSpec((1, d), lambda i, j: (0, 0)),
                  pl.BlockSpec((d, tn), lambda i, j: (0, j)),
                  pl.BlockSpec((8, tn), lambda i, j: (0, j)),
                  pl.BlockSpec((1, tn), lambda i, j: (0, j))],
        out_specs=pl.BlockSpec((tm, tn), lambda i, j: (i, j)),
        out_shape=jax.ShapeDtypeStruct((m, wp), F32),
        scratch_shapes=[pltpu.VMEM((tm, d), BF16)],
        compiler_params=_params("parallel", "arbitrary"),
    )(h2d, shift, scale, pre_w, w, cw, cb)


def _direction_masks(d):
    ri = lax.broadcasted_iota(jnp.int32, (CHUNK, CHUNK), 0)
    ci = lax.broadcasted_iota(jnp.int32, (CHUNK, CHUNK), 1)
    diff = (ri - ci) * (1 - 2 * d)
    return diff >= 0, diff > 0, ri, ci


def _chunk_index(d, p, nc):
    return p + d * (nc - 1 - 2 * p)


def _ssd_kernel(xs_ref, bc_ref, sm_ref, par_ref, dsk_ref, h0_ref, y_ref, hT_ref):
    d = pl.program_id(1)
    p = pl.program_id(2)
    heads_per_group = xs_ref.shape[-1] // (SSD_GROUPS * SSD_HEAD_DIM)
    gw = heads_per_group * SSD_HEAD_DIM

    @pl.when(p == 0)
    def _():
        hT_ref[...] = h0_ref[...]

    incl, _, _, _ = _direction_masks(d)
    fwd = d == 0
    sm = sm_ref[0]
    par = par_ref[0]
    dt = _softplus(sm + par[0:1])
    loga = dt * par[1:2]
    acum = jnp.dot(incl.astype(F32), loga, precision=HIGHEST, preferred_element_type=F32)
    acum_t = acum.T
    dt_t = dt.T
    alast = jnp.where(fwd, acum[CHUNK - 1:CHUNK], acum[0:1])
    eacum = jnp.exp(acum)
    wdec = dt * jnp.exp(alast - acum)
    cdec = jnp.exp(alast)
    skip = dsk_ref[...] * (1 - d).astype(F32)

    xs = xs_ref[0]
    bc = bc_ref[0]
    for g in range(SSD_GROUPS):
        bg = bc[:, g * SSD_STATE:(g + 1) * SSD_STATE].astype(BF16)
        cg = bc[:, (SSD_GROUPS + g) * SSD_STATE:(SSD_GROUPS + g + 1) * SSD_STATE].astype(BF16)
        cb = _dot_nt(cg, bg)
        h_t = hT_ref[0, 0, g]
        yoff = _dot(cg, h_t.astype(BF16))
        ys, xw, cds = [], [], []
        for r in range(heads_per_group):
            hh = g * heads_per_group + r
            seg = acum[:, hh:hh + 1] - acum_t[hh:hh + 1, :]
            lmat = jnp.where(incl, jnp.exp(seg), 0.0)
            mm = (cb * lmat * dt_t[hh:hh + 1, :]).astype(BF16)
            xh = xs[:, hh * SSD_HEAD_DIM:(hh + 1) * SSD_HEAD_DIM]
            yd = _dot(mm, xh.astype(BF16))
            ys.append(yd + yoff[:, r * SSD_HEAD_DIM:(r + 1) * SSD_HEAD_DIM] * eacum[:, hh:hh + 1])
            xw.append(xh * wdec[:, hh:hh + 1])
            cds.append(jnp.broadcast_to(cdec[:, hh:hh + 1], (1, SSD_HEAD_DIM)))
        upd = _dot_tn(bg, jnp.concatenate(xw, axis=1).astype(BF16))
        hT_ref[0, 0, g] = h_t * jnp.concatenate(cds, axis=1) + upd
        xg = xs[:, g * gw:(g + 1) * gw]
        y_ref[0, 0, :, g * gw:(g + 1) * gw] = jnp.concatenate(ys, axis=1) + xg * skip[:, g * gw:(g + 1) * gw]


def _ssd_scan(p3d, par, dsk, h0, *, cols):
    bsz, length, _ = p3d.shape
    nc = length // CHUNK
    c_xs, c_bc, c_sm, w_xs, w_bc = cols
    chunk = functools.partial(_chunk_index, nc=nc)
    state_shape = h0.shape
    return pl.pallas_call(
        _ssd_kernel,
        name="ssd_scan",
        grid=(bsz, N_DIR, nc),
        in_specs=[pl.BlockSpec((1, CHUNK, w_xs), lambda b, d, p: (b, chunk(d, p), c_xs // w_xs)),
                  pl.BlockSpec((1, CHUNK, w_bc), lambda b, d, p: (b, chunk(d, p), c_bc // w_bc)),
                  pl.BlockSpec((1, CHUNK, LANE), lambda b, d, p: (b, chunk(d, p), c_sm // LANE + d)),
                  pl.BlockSpec((1, 8, LANE), lambda b, d, p: (d, 0, 0)),
                  pl.BlockSpec((1, w_xs), lambda b, d, p: (0, 0)),
                  pl.BlockSpec((1, 1) + state_shape[2:], lambda b, d, p: (b, d, 0, 0, 0))],
        out_specs=[pl.BlockSpec((1, 1, CHUNK, w_xs), lambda b, d, p: (d, b, chunk(d, p), 0)),
                   pl.BlockSpec((1, 1) + state_shape[2:], lambda b, d, p: (b, d, 0, 0, 0))],
        out_shape=[jax.ShapeDtypeStruct((N_DIR, bsz, length, w_xs), F32),
                   jax.ShapeDtypeStruct(state_shape, F32)],
        compiler_params=_params("parallel", "parallel", "arbitrary"),
    )(p3d, p3d, p3d, par, dsk, h0)


def _unit_triangular_inverse(a_list, xor_ij):
    ri = lax.broadcasted_iota(jnp.int32, xor_ij.shape, 0)
    ci = lax.broadcasted_iota(jnp.int32, xor_ij.shape, 1)
    eye = jnp.where(ri == ci, 1.0, 0.0)
    in_pair = xor_ij < 2
    t_list = [eye - jnp.where(in_pair, a, 0.0) for a in a_list]
    size = 2
    while size < CHUNK:
        couples = (xor_ij >> (size.bit_length() - 1)) == 1
        a_off = [jnp.where(couples, a, 0.0).astype(BF16) for a in a_list]
        t16 = [t.astype(BF16) for t in t_list]
        ta = [_dot(t, a).astype(BF16) for t, a in zip(t16, a_off)]
        t_list = [t - _dot(x, tb) for t, x, tb in zip(t_list, ta, t16)]
        size *= 2
    return t_list


def _gdn_kernel(q_ref, k_ref, v_ref, sm_ref, par_ref, s0_ref, o_ref, s_ref, *, beta_lane, g_lane):
    d = pl.program_id(1)
    p = pl.program_id(2)
    heads = range(q_ref.shape[-1] // GDN_HEAD_DIM)

    @pl.when(p == 0)
    def _():
        s_ref[...] = s0_ref[...]

    incl, strict, ri, ci = _direction_masks(d)
    xor_ij = ri ^ ci
    fwd = d == 0
    sm = sm_ref[0]
    par = par_ref[0]
    beta_all = _sigmoid(sm)
    g_all = par[1:2] * _softplus(sm + par[0:1])
    gcum = jnp.dot(incl.astype(F32), g_all, precision=HIGHEST, preferred_element_type=F32)
    gcum_t = gcum.T
    glast = jnp.where(fwd, gcum[CHUNK - 1:CHUNK], gcum[0:1])
    eg = jnp.exp(gcum)
    eend = jnp.exp(glast - gcum)
    cdec = jnp.exp(glast)

    def head_cols(ref, h):
        return ref[0, :, h * GDN_HEAD_DIM:(h + 1) * GDN_HEAD_DIM]

    def col(x, lane):
        return x[:, lane:lane + 1]

    qn, kn = [], []
    for h in heads:
        q = head_cols(q_ref, h)
        k = head_cols(k_ref, h)
        qn.append(q * (lax.rsqrt(jnp.sum(q * q, axis=-1, keepdims=True) + EPS) * GDN_HEAD_DIM ** -0.5))
        kn.append(k * lax.rsqrt(jnp.sum(k * k, axis=-1, keepdims=True) + EPS))
    beta = [col(beta_all, beta_lane + h) for h in heads]
    dec = [jnp.where(incl, jnp.exp(col(gcum, g_lane + h) - gcum_t[g_lane + h:g_lane + h + 1, :]), 0.0)
           for h in heads]
    kb = [kn[h] * beta[h] for h in heads]
    k16 = [x.astype(BF16) for x in kn]
    a = [jnp.where(strict, _dot_nt(kb[h].astype(BF16), k16[h]) * dec[h], 0.0) for h in heads]
    attn = [(_dot_nt(qn[h].astype(BF16), k16[h]) * dec[h]).astype(BF16) for h in heads]
    t = _unit_triangular_inverse(a, xor_ij)
    rhs = [jnp.concatenate([head_cols(v_ref, h) * beta[h], kb[h] * col(eg, g_lane + h)], axis=1).astype(BF16)
           for h in heads]
    sol = [_dot(t[h].astype(BF16), rhs[h]) for h in heads]
    wq = [jnp.concatenate([sol[h][:, GDN_HEAD_DIM:].astype(BF16),
                           (qn[h] * col(eg, g_lane + h)).astype(BF16)], axis=0) for h in heads]
    ke = [(kn[h] * col(eend, g_lane + h)).astype(BF16) for h in heads]
    s = [s_ref[0, 0, h] for h in heads]
    ws = [_dot(wq[h], s[h].astype(BF16)) for h in heads]
    vnew = [(sol[h][:, :GDN_HEAD_DIM] - ws[h][:CHUNK]).astype(BF16) for h in heads]
    o = [ws[h][CHUNK:] + _dot(attn[h], vnew[h]) for h in heads]
    s_new = [s[h] * col(cdec, g_lane + h) + _dot_tn(ke[h], vnew[h]) for h in heads]
    for h in heads:
        o_ref[0, 0, :, h * GDN_HEAD_DIM:(h + 1) * GDN_HEAD_DIM] = o[h]
        s_ref[0, 0, h] = s_new[h]


def _gdn_scan(p3d, par, s0, *, cols, beta_lane, g_lane):
    bsz, length, _ = p3d.shape
    nc = length // CHUNK
    c_q, c_sm, w_q = cols
    chunk = functools.partial(_chunk_index, nc=nc)
    state_shape = s0.shape
    kern = functools.partial(_gdn_kernel, beta_lane=beta_lane, g_lane=g_lane)
    qkv_spec = [pl.BlockSpec((1, CHUNK, w_q), functools.partial(
        lambda b, d, p, i: (b, chunk(d, p), c_q // w_q + i), i=i)) for i in range(3)]
    return pl.pallas_call(
        kern,
        name="gdn_scan",
        grid=(bsz, N_DIR, nc),
        in_specs=qkv_spec + [
            pl.BlockSpec((1, CHUNK, LANE), lambda b, d, p: (b, chunk(d, p), c_sm // LANE + d)),
            pl.BlockSpec((1, 8, LANE), lambda b, d, p: (d, 0, 0)),
            pl.BlockSpec((1, 1) + state_shape[2:], lambda b, d, p: (b, d, 0, 0, 0))],
        out_specs=[pl.BlockSpec((1, 1, CHUNK, w_q), lambda b, d, p: (d, b, chunk(d, p), 0)),
                   pl.BlockSpec((1, 1) + state_shape[2:], lambda b, d, p: (b, d, 0, 0, 0))],
        out_shape=[jax.ShapeDtypeStruct((N_DIR, bsz, length, w_q), F32),
                   jax.ShapeDtypeStruct(state_shape, F32)],
        compiler_params=_params("parallel", "parallel", "arbitrary"),
    )(p3d, p3d, p3d, p3d, par, s0)


def _outproj_kernel(y0_ref, y1_ref, o0_ref, o1_ref, z_ref, h_ref, gate_ref, w_ref,
                    snw_ref, gnw_ref, postw_ref, out_ref):
    zs = _silu(z_ref[...])
    ssd_w = y0_ref.shape[-1]
    gw = ssd_w // SSD_GROUPS
    ys = (y0_ref[0] + y1_ref[0]) * zs[:, :ssd_w]
    parts = []
    for g in range(SSD_GROUPS):
        yg = ys[:, g * gw:(g + 1) * gw]
        parts.append(yg * lax.rsqrt(jnp.mean(yg * yg, axis=-1, keepdims=True) + EPS))
    ysn = jnp.concatenate(parts, axis=1) * snw_ref[...]
    og = o0_ref[0] + o1_ref[0]
    parts = []
    for h in range(og.shape[-1] // GDN_HEAD_DIM):
        oh = og[:, h * GDN_HEAD_DIM:(h + 1) * GDN_HEAD_DIM]
        parts.append(oh * lax.rsqrt(jnp.mean(oh * oh, axis=-1, keepdims=True) + EPS))
    ogn = jnp.concatenate(parts, axis=1) * gnw_ref[...] * zs[:, ssd_w:]
    ycat = jnp.concatenate([ysn, ogn], axis=1).astype(BF16)
    m = _dot(ycat, w_ref[...])
    mn = m * lax.rsqrt(jnp.mean(m * m, axis=-1, keepdims=True) + EPS) * postw_ref[...]
    out_ref[...] = h_ref[...] + gate_ref[0, 0] * mn


def _outproj(y, o, p2d, h2d, gate, w, snw, gnw, postw, *, rows_per_mod, tm):
    m, d = h2d.shape
    ssd_w = y.shape[-1]
    gdn_w = o.shape[-1]
    tiles_per_mod = rows_per_mod // tm
    y2 = y.reshape(N_DIR, m, ssd_w)
    o2 = o.reshape(N_DIR, m, gdn_w)
    return pl.pallas_call(
        _outproj_kernel,
        name="outproj",
        grid=(m // tm,),
        in_specs=[pl.BlockSpec((1, tm, ssd_w), lambda i: (0, i, 0)),
                  pl.BlockSpec((1, tm, ssd_w), lambda i: (1, i, 0)),
                  pl.BlockSpec((1, tm, gdn_w), lambda i: (0, i, 0)),
                  pl.BlockSpec((1, tm, gdn_w), lambda i: (1, i, 0)),
                  pl.BlockSpec((tm, ssd_w + gdn_w), lambda i: (i, 0)),
                  pl.BlockSpec((tm, d), lambda i: (i, 0)),
                  pl.BlockSpec((1, 1, 1, d), lambda i: (i // tiles_per_mod, 0, 0, 2)),
                  pl.BlockSpec((ssd_w + gdn_w, d), lambda i: (0, 0)),
                  pl.BlockSpec((1, ssd_w), lambda i: (0, 0)),
                  pl.BlockSpec((1, gdn_w), lambda i: (0, 0)),
                  pl.BlockSpec((1, d), lambda i: (0, 0))],
        out_specs=pl.BlockSpec((tm, d), lambda i: (i, 0)),
        out_shape=jax.ShapeDtypeStruct((m, d), F32),
        compiler_params=_params("parallel"),
    )(y2, y2, o2, o2, p2d, h2d, gate, w, snw, gnw, postw)


def kernel(x, c, ctx, c_ctx, w_ada, b_ada, pre_norm_w, post_norm_w, w_in, conv_ssd_w, conv_ssd_b,
           conv_gdn_w, ssd_a_log, ssd_dt_bias, ssd_d, ssd_norm_w, gdn_a_log, gdn_dt_bias,
           gdn_norm_w, w_out):
    bsz, seq, d = x.shape
    ctx_len = ctx.shape[1]
    depth = w_in.shape[0]
    d_mix = w_out.shape[1]
    ssd_w = d_mix // 2
    gdn_w = d_mix - ssd_w
    ssd_heads = ssd_w // SSD_HEAD_DIM
    gdn_heads = gdn_w // GDN_HEAD_DIM
    bc_w = 2 * SSD_GROUPS * SSD_STATE
    assert ssd_w == gdn_w and ssd_heads + 2 * gdn_heads <= LANE
    assert seq % CHUNK == 0 and ctx_len % CHUNK == 0 and CHUNK % GRID_W == 0

    c_z, c_xs = 0, d_mix
    c_qkv = c_xs + ssd_w
    c_bc = c_qkv + 3 * gdn_w
    c_sm = c_bc + bc_w
    tn = 512
    wp = -(-(c_sm + N_DIR * LANE) // tn) * tn
    assert c_qkv % (3 * gdn_w) == 0 or c_qkv % gdn_w == 0
    assert c_xs % ssd_w == 0 and c_bc % bc_w == 0 and c_xs % tn == 0 and c_sm % tn == 0
    beta_lane, g_lane = ssd_heads, ssd_heads + gdn_heads

    o_xbc = d_mix
    o_dt = o_xbc + ssd_w + bc_w
    o_qkv = o_dt + N_DIR * ssd_heads
    o_beta = o_qkv + 3 * gdn_w
    o_alpha = o_beta + N_DIR * gdn_heads

    def small_cols(dd):
        parts = [w_in[:, :, o_dt + dd * ssd_heads:o_dt + (dd + 1) * ssd_heads],
                 w_in[:, :, o_beta + dd * gdn_heads:o_beta + (dd + 1) * gdn_heads],
                 w_in[:, :, o_alpha + dd * gdn_heads:o_alpha + (dd + 1) * gdn_heads]]
        used = ssd_heads + 2 * gdn_heads
        return parts + [jnp.zeros((depth, d, LANE - used), w_in.dtype)]

    w_cat = jnp.concatenate(
        [w_in[:, :, :d_mix], w_in[:, :, o_xbc:o_xbc + ssd_w], w_in[:, :, o_qkv:o_qkv + 3 * gdn_w],
         w_in[:, :, o_xbc + ssd_w:o_xbc + ssd_w + bc_w]] + small_cols(0) + small_cols(1)
        + [jnp.zeros((depth, d, wp - c_sm - N_DIR * LANE), w_in.dtype)], axis=-1).astype(BF16)

    def conv_layout(ssd_part, gdn_part):
        lead = ssd_part.shape[:-1]
        return jnp.concatenate(
            [jnp.zeros(lead + (d_mix,), F32), ssd_part[..., :ssd_w], gdn_part, ssd_part[..., ssd_w:],
             jnp.zeros(lead + (wp - c_sm,), F32)], axis=-1)

    cw_all = conv_layout(conv_ssd_w.astype(F32), conv_gdn_w.astype(F32))
    cw_all = jnp.concatenate([cw_all, jnp.zeros((depth, 8 - CONV_K, wp), F32)], axis=1)
    cb_all = conv_layout(conv_ssd_b.astype(F32)[:, None], jnp.zeros((depth, 1, 3 * gdn_w), F32))

    def lane_row(vals, lane0):
        return jnp.pad(vals, ((0, 0), (0, 0), (lane0, LANE - lane0 - vals.shape[-1])))

    zeros_rows = jnp.zeros((depth, N_DIR, 6, LANE), F32)
    ssd_par = jnp.concatenate([lane_row(ssd_dt_bias.astype(F32), 0)[:, :, None],
                               lane_row(-jnp.exp(ssd_a_log.astype(F32)), 0)[:, :, None], zeros_rows], axis=2)
    gdn_par = jnp.concatenate([lane_row(gdn_dt_bias.astype(F32), g_lane)[:, :, None],
                               lane_row(-jnp.exp(gdn_a_log.astype(F32)), g_lane)[:, :, None], zeros_rows], axis=2)
    dsk_all = jnp.repeat(ssd_d.astype(F32), SSD_HEAD_DIM, axis=-1)[:, None]
    gnw_all = jnp.tile(gdn_norm_w.astype(F32), (1, gdn_heads))[:, None]
    w_out16 = w_out.astype(BF16)

    rows = 16
    c_all = jnp.concatenate([c, c_ctx[None], jnp.zeros((rows - bsz - 1, d), c.dtype)], axis=0)
    mod = _modulation(c_all, w_ada, b_ada).reshape(depth, rows, 1, 3 * d)

    h_lat = x.reshape(bsz * seq, d)
    h_ctx = ctx.reshape(bsz * ctx_len, d)
    tm_lat = min(1024, seq)
    tm_ctx = min(1024, bsz * ctx_len)
    tm_out = 256
    conv_cols = (c_xs, c_sm)
    ssd_cols = (c_xs, c_bc, c_sm, ssd_w, bc_w)
    gdn_cols = (c_qkv, c_sm, gdn_w)
    ssd_state0 = jnp.zeros((bsz, N_DIR, SSD_GROUPS, SSD_STATE, ssd_w // SSD_GROUPS), F32)
    gdn_state0 = jnp.zeros((bsz, N_DIR, gdn_heads, GDN_HEAD_DIM, GDN_HEAD_DIM), F32)

    for l in range(depth):
        mod_lat = mod[l, :bsz, None]
        mod_ctx = mod[l, bsz:bsz + 1, None]
        pre_w = pre_norm_w[l][None]
        p_lat = _inproj(h_lat, mod_lat, mod_lat, pre_w, w_cat[l], cw_all[l], cb_all[l],
                        rows_per_mod=seq, group=GRID_W, tm=tm_lat, tn=tn, conv_cols=conv_cols)
        p_ctx = _inproj(h_ctx, mod_ctx, mod_ctx, pre_w, w_cat[l], cw_all[l], cb_all[l],
                        rows_per_mod=bsz * ctx_len, group=ctx_len, tm=tm_ctx, tn=tn, conv_cols=conv_cols)
        p_lat3 = p_lat.reshape(bsz, seq, wp)
        p_ctx3 = p_ctx.reshape(bsz, ctx_len, wp)

        y_ctx, ssd_state = _ssd_scan(p_ctx3, ssd_par[l], dsk_all[l], ssd_state0, cols=ssd_cols)
        y_lat, _ = _ssd_scan(p_lat3, ssd_par[l], dsk_all[l], ssd_state, cols=ssd_cols)
        o_ctx, gdn_state = _gdn_scan(p_ctx3, gdn_par[l], gdn_state0, cols=gdn_cols,
                                     beta_lane=beta_lane, g_lane=g_lane)
        o_lat, _ = _gdn_scan(p_lat3, gdn_par[l], gdn_state, cols=gdn_cols,
                             beta_lane=beta_lane, g_lane=g_lane)

        snw = ssd_norm_w[l][None].astype(F32)
        post_w = post_norm_w[l][None]
        h_lat_new = _outproj(y_lat, o_lat, p_lat, h_lat, mod_lat, w_out16[l], snw, gnw_all[l], post_w,
                             rows_per_mod=seq, tm=min(tm_out, seq))
        if l < depth - 1:
            h_ctx = _outproj(y_ctx, o_ctx, p_ctx, h_ctx, mod_ctx, w_out16[l], snw, gnw_all[l], post_w,
                             rows_per_mod=bsz * ctx_len, tm=min(tm_out, bsz * ctx_len))
        h_lat = h_lat_new
    return h_lat.reshape(bsz, seq, d)
```

```python
import functools

import jax
import jax.numpy as jnp
from jax import lax
from jax.experimental import pallas as pl
from jax.experimental.pallas import tpu as pltpu

F32 = jnp.float32
BF16 = jnp.bfloat16
HIGHEST = lax.Precision.HIGHEST

EPS = 1e-6
CHUNK = 128
GRID_W = 64
CONV_K = 5
N_DIR = 2
SSD_HEAD_DIM = 64
SSD_STATE = 128
SSD_GROUPS = 2
GDN_HEAD_DIM = 128
LANE = 128
SUBLANE = 8
NORM_UNROLL = 8
GDN_CHUNKS_PER_STEP = 2
SUB_ROWS = 256
NORM_ROWS = 16
VMEM_LIMIT = 52 * 1024 * 1024

NT = (((1,), (1,)), ((), ()))
TN = (((0,), (0,)), ((), ()))


def _dot(a, b):
    return jnp.dot(a, b, preferred_element_type=F32)


def _dot_nt(a, b):
    return lax.dot_general(a, b, NT, preferred_element_type=F32)


def _dot_tn(a, b):
    return lax.dot_general(a, b, TN, preferred_element_type=F32)


def _sigmoid(x):
    return 1.0 / (1.0 + jnp.exp(-x))


def _silu(x):
    half = 0.5 * x
    return half + half * jnp.tanh(half)


def _softplus(x):
    return jnp.maximum(x, 0.0) + jnp.log(1.0 + jnp.exp(-jnp.abs(x)))


def _params(*sem):
    return pltpu.CompilerParams(dimension_semantics=sem, vmem_limit_bytes=VMEM_LIMIT)


def _mod_kernel(c_ref, w_ref, b_ref, o_ref):
    a = _silu(c_ref[...])
    o_ref[0] = jnp.dot(a, w_ref[0], precision=HIGHEST, preferred_element_type=F32) + b_ref[0]


def _modulation(c_all, w_ada, b_ada):
    depth, d, n = w_ada.shape
    rows = c_all.shape[0]
    tn = 768
    return pl.pallas_call(
        _mod_kernel,
        name="adaln_mod",
        grid=(depth, n // tn),
        in_specs=[pl.BlockSpec((rows, d), lambda l, j: (0, 0)),
                  pl.BlockSpec((1, d, tn), lambda l, j: (l, 0, j)),
                  pl.BlockSpec((1, 1, tn), lambda l, j: (l, 0, j))],
        out_specs=pl.BlockSpec((1, rows, tn), lambda l, j: (l, 0, j)),
        out_shape=jax.ShapeDtypeStruct((depth, rows, n), F32),
        compiler_params=_params("parallel", "parallel"),
    )(c_all, w_ada, b_ada.reshape(depth, 1, n))


def _inproj_kernel(h_ref, shift_ref, scale_ref, prew_ref, w_ref, cw_ref, cb_ref, o_ref, u_ref,
                   *, group, j_lo, j_hi):
    j = pl.program_id(1)
    tm, tn = o_ref.shape
    sub = max(group, min(tm, SUB_ROWS))
    is_conv = (j >= j_lo) & (j < j_hi)

    @pl.when(j == 0)
    def _():
        prew = prew_ref[...]
        scale1p = 1.0 + scale_ref[0, 0]
        shift = shift_ref[0, 0]

        def norm_strip(s, carry):
            r0 = pl.multiple_of(s * NORM_ROWS, NORM_ROWS)
            x = h_ref[pl.ds(r0, NORM_ROWS), :]
            y = x * lax.rsqrt(jnp.mean(x * x, axis=-1, keepdims=True) + EPS) * prew
            u_ref[pl.ds(r0, NORM_ROWS), :] = (y * scale1p + shift).astype(BF16)
            return carry

        lax.fori_loop(0, tm // NORM_ROWS, norm_strip, 0, unroll=NORM_UNROLL)

    @pl.when(jnp.logical_not(is_conv))
    def _():
        o_ref[...] = _dot(u_ref[...], w_ref[...])

    @pl.when(is_conv)
    def _():
        cw = cw_ref[...]
        row = lax.broadcasted_iota(jnp.int32, (SUBLANE, tn), 0)
        offs = [k - CONV_K // 2 for k in range(CONV_K)]
        w_mid = [cw[k:k + 1] for k in range(CONV_K)]
        w_first = [jnp.where(row >= -off, w, 0.0) if off < 0 else w for off, w in zip(offs, w_mid)]
        w_last = [jnp.where(row < SUBLANE - off, w, 0.0) if off > 0 else w for off, w in zip(offs, w_mid)]
        bias = cb_ref[...]
        last0 = group - SUBLANE

        def taps(pieces, weights, lo, hi):
            out = bias
            for x, w in zip(pieces, weights):
                out = out + x[lo:hi] * w
            return _silu(out)

        n_sub = tm // sub
        acc_next = _dot(u_ref[0:sub, :], w_ref[...])
        for sb in range(n_sub):
            acc = acc_next
            if sb + 1 < n_sub:
                acc_next = _dot(u_ref[(sb + 1) * sub:(sb + 2) * sub, :], w_ref[...])
            shifted = [acc if off == 0 else pltpu.roll(acc, (-off) % sub, axis=0) for off in offs]
            for gi in range(sub // group):
                g0 = gi * group
                o0 = sb * sub + g0
                o_ref[o0:o0 + SUBLANE, :] = taps(shifted, w_first, g0, g0 + SUBLANE)
                o_ref[o0 + SUBLANE:o0 + last0, :] = taps(shifted, w_mid, g0 + SUBLANE, g0 + last0)
                o_ref[o0 + last0:o0 + group, :] = taps(shifted, w_last, g0 + last0, g0 + group)


def _inproj(h2d, shift, scale, pre_w, w, cw, cb, *, rows_per_mod, group, tm, tn, conv_cols):
    m, d = h2d.shape
    wp = w.shape[1]
    tiles_per_mod = rows_per_mod // tm
    kern = functools.partial(_inproj_kernel, group=group, j_lo=conv_cols[0] // tn, j_hi=conv_cols[1] // tn)
    return pl.pallas_call(
        kern,
        name="inproj",
        grid=(m // tm, wp // tn),
        in_specs=[pl.BlockSpec((tm, d), lambda i, j: (i, 0)),
                  pl.BlockSpec((1, 1, 1, d), lambda i, j: (i // tiles_per_mod, 0, 0, 0)),
                  pl.BlockSpec((1, 1, 1, d), lambda i, j: (i // tiles_per_mod, 0, 0, 1)),
                  pl.BlockSpec((1, d), lambda i, j: (0, 0)),
                  pl.BlockSpec((d, tn), lambda i, j: (0, j)),
                  pl.BlockSpec((8, tn), lambda i, j: (0, j)),
                  pl.BlockSpec((1, tn), lambda i, j: (0, j))],
        out_specs=pl.BlockSpec((tm, tn), lambda i, j: (i, j)),
        out_shape=jax.ShapeDtypeStruct((m, wp), F32),
        scratch_shapes=[pltpu.VMEM((tm, d), BF16)],
        compiler_params=_params("parallel", "arbitrary"),
    )(h2d, shift, scale, pre_w, w, cw, cb)


def _direction_masks(d):
    ri = lax.broadcasted_iota(jnp.int32, (CHUNK, CHUNK), 0)
    ci = lax.broadcasted_iota(jnp.int32, (CHUNK, CHUNK), 1)
    diff = (ri - ci) * (1 - 2 * d)
    return diff >= 0, diff > 0, ri, ci


def _chunk_index(d, p, nc):
    return p + d * (nc - 1 - 2 * p)


def _ssd_kernel(xs_ref, bc_ref, sm_ref, par_ref, dsk_ref, h0_ref, y_ref, hT_ref):
    d = pl.program_id(1)
    p = pl.program_id(2)
    heads_per_group = xs_ref.shape[-1] // (SSD_GROUPS * SSD_HEAD_DIM)
    gw = heads_per_group * SSD_HEAD_DIM

    @pl.when(p == 0)
    def _():
        hT_ref[...] = h0_ref[...]

    incl, _, _, _ = _direction_masks(d)
    fwd = d == 0
    sm = sm_ref[0]
    par = par_ref[0]
    dt = _softplus(sm + par[0:1])
    loga = dt * par[1:2]
    acum = jnp.dot(incl.astype(F32), loga, precision=HIGHEST, preferred_element_type=F32)
    acum_t = acum.T
    dt_t = dt.T
    alast = jnp.where(fwd, acum[CHUNK - 1:CHUNK], acum[0:1])
    eacum = jnp.exp(acum)
    wdec = dt * jnp.exp(alast - acum)
    cdec = jnp.exp(alast)
    skip = dsk_ref[...] * (1 - d).astype(F32)

    xs = xs_ref[0]
    bc = bc_ref[0]
    groups = range(SSD_GROUPS)
    ppg = gw // LANE
    pairs = range(SSD_GROUPS * ppg)
    left = lax.broadcasted_iota(jnp.int32, (CHUNK, LANE), 1) < SSD_HEAD_DIM

    def col(x, lane_idx):
        return x[:, lane_idx:lane_idx + 1]

    def pair_cols(x, i):
        return jnp.where(left[:x.shape[0]], col(x, 2 * i), col(x, 2 * i + 1))

    def intra(hh):
        seg = col(acum, hh) - acum_t[hh:hh + 1, :]
        return (cb[hh // heads_per_group] * jnp.where(incl, jnp.exp(seg), 0.0) * dt_t[hh:hh + 1, :]).astype(BF16)

    bg = [bc[:, g * SSD_STATE:(g + 1) * SSD_STATE].astype(BF16) for g in groups]
    cg = [bc[:, (SSD_GROUPS + g) * SSD_STATE:(SSD_GROUPS + g + 1) * SSD_STATE].astype(BF16) for g in groups]
    cb = [_dot_nt(cg[g], bg[g]) for g in groups]
    h_t = [hT_ref[0, 0, g] for g in groups]
    yoff = [_dot(cg[g], h_t[g].astype(BF16)) for g in groups]
    xp = [xs[:, i * LANE:(i + 1) * LANE] for i in pairs]
    m_pair = [jnp.concatenate([intra(2 * i), intra(2 * i + 1)], axis=1) for i in pairs]
    x_bd = [jnp.concatenate([jnp.where(left, x, 0.0), jnp.where(left, 0.0, x)], axis=0).astype(BF16) for x in xp]
    yd = [_dot(m_pair[i], x_bd[i]) for i in pairs]
    xw = [(xp[i] * pair_cols(wdec, i)).astype(BF16) for i in pairs]
    upd = [_dot_tn(bg[g], jnp.concatenate(xw[g * ppg:(g + 1) * ppg], axis=1)) for g in groups]
    for g in groups:
        cd = jnp.concatenate([pair_cols(cdec, i) for i in range(g * ppg, (g + 1) * ppg)], axis=1)
        hT_ref[0, 0, g] = h_t[g] * cd + upd[g]
    for i in pairs:
        lo = i * LANE
        yo = yoff[i // ppg][:, (i % ppg) * LANE:(i % ppg + 1) * LANE]
        y_ref[0, 0, :, lo:lo + LANE] = yd[i] + yo * pair_cols(eacum, i) + xp[i] * skip[:, lo:lo + LANE]


def _ssd_scan(p3d, par, dsk, h0, *, cols):
    bsz, length, _ = p3d.shape
    nc = length // CHUNK
    c_xs, c_bc, c_sm, w_xs, w_bc = cols
    chunk = functools.partial(_chunk_index, nc=nc)
    state_shape = h0.shape
    return pl.pallas_call(
        _ssd_kernel,
        name="ssd_scan",
        grid=(bsz, N_DIR, nc),
        in_specs=[pl.BlockSpec((1, CHUNK, w_xs), lambda b, d, p: (b, chunk(d, p), c_xs // w_xs)),
                  pl.BlockSpec((1, CHUNK, w_bc), lambda b, d, p: (b, chunk(d, p), c_bc // w_bc)),
                  pl.BlockSpec((1, CHUNK, LANE), lambda b, d, p: (b, chunk(d, p), c_sm // LANE + d)),
                  pl.BlockSpec((1, 8, LANE), lambda b, d, p: (d, 0, 0)),
                  pl.BlockSpec((1, w_xs), lambda b, d, p: (0, 0)),
                  pl.BlockSpec((1, 1) + state_shape[2:], lambda b, d, p: (b, d, 0, 0, 0))],
        out_specs=[pl.BlockSpec((1, 1, CHUNK, w_xs), lambda b, d, p: (d, b, chunk(d, p), 0)),
                   pl.BlockSpec((1, 1) + state_shape[2:], lambda b, d, p: (b, d, 0, 0, 0))],
        out_shape=[jax.ShapeDtypeStruct((N_DIR, bsz, length, w_xs), F32),
                   jax.ShapeDtypeStruct(state_shape, F32)],
        compiler_params=_params("parallel", "parallel", "arbitrary"),
    )(p3d, p3d, p3d, par, dsk, h0)


def _unit_triangular_inverse(a_list, xor_ij):
    eye = jnp.where(xor_ij == 0, 1.0, 0.0)
    in_pair = xor_ij < 2
    t_list = [eye - jnp.where(in_pair, a, 0.0) for a in a_list]
    size = 2
    while size < CHUNK:
        couples = (xor_ij >> (size.bit_length() - 1)) == 1
        a_off = [jnp.where(couples, a, 0.0).astype(BF16) for a in a_list]
        t16 = [t.astype(BF16) for t in t_list]
        ta = [_dot(t, a).astype(BF16) for t, a in zip(t16, a_off)]
        t_list = [t - _dot(x, tb) for t, x, tb in zip(t_list, ta, t16)]
        size *= 2
    return t_list


def _gdn_kernel(q_ref, k_ref, v_ref, sm_ref, par_ref, s0_ref, o_ref, s_ref, *, beta_lane, g_lane):
    d = pl.program_id(1)
    p = pl.program_id(2)
    heads = range(q_ref.shape[-1] // GDN_HEAD_DIM)

    @pl.when(p == 0)
    def _():
        s_ref[...] = s0_ref[...]

    incl, strict, ri, ci = _direction_masks(d)
    xor_ij = ri ^ ci
    fwd = d == 0
    tri = incl.astype(F32)
    par = par_ref[0]
    kk = GDN_HEAD_DIM
    cps = q_ref.shape[1] // CHUNK
    rows = [pl.multiple_of((s + d * (cps - 1 - 2 * s)) * CHUNK, CHUNK) for s in range(cps)]
    units = [(s, h) for s in range(cps) for h in heads]

    def col(x, lane):
        return x[:, lane:lane + 1]

    def head_cols(ref, s, h):
        return ref[0, pl.ds(rows[s], CHUNK), h * kk:(h + 1) * kk]

    eg, eend, cdec, beta_all, gcum, gcum_t = [], [], [], [], [], []
    for s in range(cps):
        sm = sm_ref[0, pl.ds(rows[s], CHUNK), :]
        beta_all.append(_sigmoid(sm))
        g_all = par[1:2] * _softplus(sm + par[0:1])
        gc = jnp.dot(tri, g_all, precision=HIGHEST, preferred_element_type=F32)
        glast = jnp.where(fwd, gc[CHUNK - 1:CHUNK], gc[0:1])
        gcum.append(gc)
        gcum_t.append(gc.T)
        eg.append(jnp.exp(gc))
        eend.append(jnp.exp(glast - gc))
        cdec.append(jnp.exp(glast))
    qn, kn = {}, {}
    for s, h in units:
        q = head_cols(q_ref, s, h)
        k = head_cols(k_ref, s, h)
        qn[s, h] = q * (lax.rsqrt(jnp.sum(q * q, axis=-1, keepdims=True) + EPS) * kk ** -0.5)
        kn[s, h] = k * lax.rsqrt(jnp.sum(k * k, axis=-1, keepdims=True) + EPS)
    beta = {(s, h): col(beta_all[s], beta_lane + h) for s, h in units}
    dec = {(s, h): jnp.where(incl, jnp.exp(col(gcum[s], g_lane + h) - gcum_t[s][g_lane + h:g_lane + h + 1, :]), 0.0)
           for s, h in units}
    kb = {x: kn[x] * beta[x] for x in units}
    k16 = {x: kn[x].astype(BF16) for x in units}
    a = [jnp.where(strict, _dot_nt(kb[x].astype(BF16), k16[x]) * dec[x], 0.0) for x in units]
    attn = {x: (_dot_nt(qn[x].astype(BF16), k16[x]) * dec[x]).astype(BF16) for x in units}
    t = dict(zip(units, _unit_triangular_inverse(a, xor_ij)))
    sol = {(s, h): _dot(t[s, h].astype(BF16), jnp.concatenate(
        [head_cols(v_ref, s, h) * beta[s, h], kb[s, h] * col(eg[s], g_lane + h)], axis=1).astype(BF16))
           for s, h in units}
    wq = {(s, h): jnp.concatenate([sol[s, h][:, kk:].astype(BF16),
                                   (qn[s, h] * col(eg[s], g_lane + h)).astype(BF16)], axis=0) for s, h in units}
    ke = {(s, h): (kn[s, h] * col(eend[s], g_lane + h)).astype(BF16) for s, h in units}

    state = [s_ref[0, 0, h] for h in heads]
    for s in range(cps):
        ws = [_dot(wq[s, h], state[h].astype(BF16)) for h in heads]
        vnew = [(sol[s, h][:, :kk] - ws[h][:CHUNK]).astype(BF16) for h in heads]
        o = [ws[h][CHUNK:] + _dot(attn[s, h], vnew[h]) for h in heads]
        state = [state[h] * col(cdec[s], g_lane + h) + _dot_tn(ke[s, h], vnew[h]) for h in heads]
        for h in heads:
            o_ref[0, 0, pl.ds(rows[s], CHUNK), h * kk:(h + 1) * kk] = o[h]
    for h in heads:
        s_ref[0, 0, h] = state[h]


def _gdn_scan(p3d, par, s0, *, cols, beta_lane, g_lane):
    bsz, length, _ = p3d.shape
    nc = length // CHUNK
    c_q, c_sm, w_q = cols
    cps = min(GDN_CHUNKS_PER_STEP, nc)
    nb = nc // cps
    rows = cps * CHUNK
    chunk = functools.partial(_chunk_index, nc=nb)
    state_shape = s0.shape
    kern = functools.partial(_gdn_kernel, beta_lane=beta_lane, g_lane=g_lane)
    qkv_spec = [pl.BlockSpec((1, rows, w_q), functools.partial(
        lambda b, d, p, i: (b, chunk(d, p), c_q // w_q + i), i=i)) for i in range(3)]
    return pl.pallas_call(
        kern,
        name="gdn_scan",
        grid=(bsz, N_DIR, nb),
        in_specs=qkv_spec + [
            pl.BlockSpec((1, rows, LANE), lambda b, d, p: (b, chunk(d, p), c_sm // LANE + d)),
            pl.BlockSpec((1, 8, LANE), lambda b, d, p: (d, 0, 0)),
            pl.BlockSpec((1, 1) + state_shape[2:], lambda b, d, p: (b, d, 0, 0, 0))],
        out_specs=[pl.BlockSpec((1, 1, rows, w_q), lambda b, d, p: (d, b, chunk(d, p), 0)),
                   pl.BlockSpec((1, 1) + state_shape[2:], lambda b, d, p: (b, d, 0, 0, 0))],
        out_shape=[jax.ShapeDtypeStruct((N_DIR, bsz, length, w_q), F32),
                   jax.ShapeDtypeStruct(state_shape, F32)],
        compiler_params=_params("parallel", "parallel", "arbitrary"),
    )(p3d, p3d, p3d, p3d, par, s0)


def _outproj_kernel(y0_ref, y1_ref, o0_ref, o1_ref, z_ref, h_ref, gate_ref, w_ref,
                    snw_ref, gnw_ref, postw_ref, out_ref):
    zs = _silu(z_ref[...])
    ssd_w = y0_ref.shape[-1]
    gw = ssd_w // SSD_GROUPS
    ys = (y0_ref[0] + y1_ref[0]) * zs[:, :ssd_w]
    parts = []
    for g in range(SSD_GROUPS):
        yg = ys[:, g * gw:(g + 1) * gw]
        parts.append(yg * lax.rsqrt(jnp.mean(yg * yg, axis=-1, keepdims=True) + EPS))
    ysn = jnp.concatenate(parts, axis=1) * snw_ref[...]
    og = o0_ref[0] + o1_ref[0]
    parts = []
    for h in range(og.shape[-1] // GDN_HEAD_DIM):
        oh = og[:, h * GDN_HEAD_DIM:(h + 1) * GDN_HEAD_DIM]
        parts.append(oh * lax.rsqrt(jnp.mean(oh * oh, axis=-1, keepdims=True) + EPS))
    ogn = jnp.concatenate(parts, axis=1) * gnw_ref[...] * zs[:, ssd_w:]
    ycat = jnp.concatenate([ysn, ogn], axis=1).astype(BF16)
    m = _dot(ycat, w_ref[...])
    mn = m * lax.rsqrt(jnp.mean(m * m, axis=-1, keepdims=True) + EPS) * postw_ref[...]
    out_ref[...] = h_ref[...] + gate_ref[0, 0] * mn


def _outproj(y, o, p2d, h2d, gate, w, snw, gnw, postw, *, rows_per_mod, tm):
    m, d = h2d.shape
    ssd_w = y.shape[-1]
    gdn_w = o.shape[-1]
    tiles_per_mod = rows_per_mod // tm
    y2 = y.reshape(N_DIR, m, ssd_w)
    o2 = o.reshape(N_DIR, m, gdn_w)
    return pl.pallas_call(
        _outproj_kernel,
        name="outproj",
        grid=(m // tm,),
        in_specs=[pl.BlockSpec((1, tm, ssd_w), lambda i: (0, i, 0)),
                  pl.BlockSpec((1, tm, ssd_w), lambda i: (1, i, 0)),
                  pl.BlockSpec((1, tm, gdn_w), lambda i: (0, i, 0)),
                  pl.BlockSpec((1, tm, gdn_w), lambda i: (1, i, 0)),
                  pl.BlockSpec((tm, ssd_w + gdn_w), lambda i: (i, 0)),
                  pl.BlockSpec((tm, d), lambda i: (i, 0)),
                  pl.BlockSpec((1, 1, 1, d), lambda i: (i // tiles_per_mod, 0, 0, 2)),
                  pl.BlockSpec((ssd_w + gdn_w, d), lambda i: (0, 0)),
                  pl.BlockSpec((1, ssd_w), lambda i: (0, 0)),
                  pl.BlockSpec((1, gdn_w), lambda i: (0, 0)),
                  pl.BlockSpec((1, d), lambda i: (0, 0))],
        out_specs=pl.BlockSpec((tm, d), lambda i: (i, 0)),
        out_shape=jax.ShapeDtypeStruct((m, d), F32),
        compiler_params=_params("parallel"),
    )(y2, y2, o2, o2, p2d, h2d, gate, w, snw, gnw, postw)


def kernel(x, c, ctx, c_ctx, w_ada, b_ada, pre_norm_w, post_norm_w, w_in, conv_ssd_w, conv_ssd_b,
           conv_gdn_w, ssd_a_log, ssd_dt_bias, ssd_d, ssd_norm_w, gdn_a_log, gdn_dt_bias,
           gdn_norm_w, w_out):
    bsz, seq, d = x.shape
    ctx_len = ctx.shape[1]
    depth = w_in.shape[0]
    d_mix = w_out.shape[1]
    ssd_w = d_mix // 2
    gdn_w = d_mix - ssd_w
    ssd_heads = ssd_w // SSD_HEAD_DIM
    gdn_heads = gdn_w // GDN_HEAD_DIM
    bc_w = 2 * SSD_GROUPS * SSD_STATE
    assert ssd_w == gdn_w and ssd_heads + 2 * gdn_heads <= LANE
    assert seq % CHUNK == 0 and ctx_len % CHUNK == 0 and CHUNK % GRID_W == 0

    c_z, c_xs = 0, d_mix
    c_qkv = c_xs + ssd_w
    c_bc = c_qkv + 3 * gdn_w
    c_sm = c_bc + bc_w
    tn = 512
    wp = -(-(c_sm + N_DIR * LANE) // tn) * tn
    assert c_qkv % (3 * gdn_w) == 0 or c_qkv % gdn_w == 0
    assert c_xs % ssd_w == 0 and c_bc % bc_w == 0 and c_xs % tn == 0 and c_sm % tn == 0
    beta_lane, g_lane = ssd_heads, ssd_heads + gdn_heads

    o_xbc = d_mix
    o_dt = o_xbc + ssd_w + bc_w
    o_qkv = o_dt + N_DIR * ssd_heads
    o_beta = o_qkv + 3 * gdn_w
    o_alpha = o_beta + N_DIR * gdn_heads

    def small_cols(dd):
        parts = [w_in[:, :, o_dt + dd * ssd_heads:o_dt + (dd + 1) * ssd_heads],
                 w_in[:, :, o_beta + dd * gdn_heads:o_beta + (dd + 1) * gdn_heads],
                 w_in[:, :, o_alpha + dd * gdn_heads:o_alpha + (dd + 1) * gdn_heads]]
        used = ssd_heads + 2 * gdn_heads
        return parts + [jnp.zeros((depth, d, LANE - used), w_in.dtype)]

    w_cat = jnp.concatenate(
        [w_in[:, :, :d_mix], w_in[:, :, o_xbc:o_xbc + ssd_w], w_in[:, :, o_qkv:o_qkv + 3 * gdn_w],
         w_in[:, :, o_xbc + ssd_w:o_xbc + ssd_w + bc_w]] + small_cols(0) + small_cols(1)
        + [jnp.zeros((depth, d, wp - c_sm - N_DIR * LANE), w_in.dtype)], axis=-1).astype(BF16)

    def conv_layout(ssd_part, gdn_part):
        lead = ssd_part.shape[:-1]
        return jnp.concatenate(
            [jnp.zeros(lead + (d_mix,), F32), ssd_part[..., :ssd_w], gdn_part, ssd_part[..., ssd_w:],
             jnp.zeros(lead + (wp - c_sm,), F32)], axis=-1)

    cw_all = conv_layout(conv_ssd_w.astype(F32), conv_gdn_w.astype(F32))
    cw_all = jnp.concatenate([cw_all, jnp.zeros((depth, 8 - CONV_K, wp), F32)], axis=1)
    cb_all = conv_layout(conv_ssd_b.astype(F32)[:, None], jnp.zeros((depth, 1, 3 * gdn_w), F32))

    def lane_row(vals, lane0):
        return jnp.pad(vals, ((0, 0), (0, 0), (lane0, LANE - lane0 - vals.shape[-1])))

    zeros_rows = jnp.zeros((depth, N_DIR, 6, LANE), F32)
    ssd_par = jnp.concatenate([lane_row(ssd_dt_bias.astype(F32), 0)[:, :, None],
                               lane_row(-jnp.exp(ssd_a_log.astype(F32)), 0)[:, :, None], zeros_rows], axis=2)
    gdn_par = jnp.concatenate([lane_row(gdn_dt_bias.astype(F32), g_lane)[:, :, None],
                               lane_row(-jnp.exp(gdn_a_log.astype(F32)), g_lane)[:, :, None], zeros_rows], axis=2)
    dsk_all = jnp.repeat(ssd_d.astype(F32), SSD_HEAD_DIM, axis=-1)[:, None]
    gnw_all = jnp.tile(gdn_norm_w.astype(F32), (1, gdn_heads))[:, None]
    w_out16 = w_out.astype(BF16)

    rows = 16
    c_all = jnp.concatenate([c, c_ctx[None], jnp.zeros((rows - bsz - 1, d), c.dtype)], axis=0)
    mod = _modulation(c_all, w_ada, b_ada).reshape(depth, rows, 1, 3 * d)

    h_lat = x.reshape(bsz * seq, d)
    h_ctx = ctx.reshape(bsz * ctx_len, d)
    tm_lat = min(1024, seq)
    tm_ctx = min(1024, bsz * ctx_len)
    tm_out = 256
    conv_cols = (c_xs, c_sm)
    ssd_cols = (c_xs, c_bc, c_sm, ssd_w, bc_w)
    gdn_cols = (c_qkv, c_sm, gdn_w)
    ssd_state0 = jnp.zeros((bsz, N_DIR, SSD_GROUPS, SSD_STATE, ssd_w // SSD_GROUPS), F32)
    gdn_state0 = jnp.zeros((bsz, N_DIR, gdn_heads, GDN_HEAD_DIM, GDN_HEAD_DIM), F32)

    for l in range(depth):
        mod_lat = mod[l, :bsz, None]
        mod_ctx = mod[l, bsz:bsz + 1, None]
        pre_w = pre_norm_w[l][None]
        p_lat = _inproj(h_lat, mod_lat, mod_lat, pre_w, w_cat[l], cw_all[l], cb_all[l],
                        rows_per_mod=seq, group=GRID_W, tm=tm_lat, tn=tn, conv_cols=conv_cols)
        p_ctx = _inproj(h_ctx, mod_ctx, mod_ctx, pre_w, w_cat[l], cw_all[l], cb_all[l],
                        rows_per_mod=bsz * ctx_len, group=ctx_len, tm=tm_ctx, tn=tn, conv_cols=conv_cols)
        p_lat3 = p_lat.reshape(bsz, seq, wp)
        p_ctx3 = p_ctx.reshape(bsz, ctx_len, wp)

        y_ctx, ssd_state = _ssd_scan(p_ctx3, ssd_par[l], dsk_all[l], ssd_state0, cols=ssd_cols)
        y_lat, _ = _ssd_scan(p_lat3, ssd_par[l], dsk_all[l], ssd_state, cols=ssd_cols)
        o_ctx, gdn_state = _gdn_scan(p_ctx3, gdn_par[l], gdn_state0, cols=gdn_cols,
                                     beta_lane=beta_lane, g_lane=g_lane)
        o_lat, _ = _gdn_scan(p_lat3, gdn_par[l], gdn_state, cols=gdn_cols,
                             beta_lane=beta_lane, g_lane=g_lane)

        snw = ssd_norm_w[l][None].astype(F32)
        post_w = post_norm_w[l][None]
        h_lat_new = _outproj(y_lat, o_lat, p_lat, h_lat, mod_lat, w_out16[l], snw, gnw_all[l], post_w,
                             rows_per_mod=seq, tm=min(tm_out, seq))
        if l < depth - 1:
            h_ctx = _outproj(y_ctx, o_ctx, p_ctx, h_ctx, mod_ctx, w_out16[l], snw, gnw_all[l], post_w,
                             rows_per_mod=bsz * ctx_len, tm=min(tm_out, bsz * ctx_len))
        h_lat = h_lat_new
    return h_lat.reshape(bsz, seq, d)
```

```python
import functools

import jax
import jax.numpy as jnp
from jax import lax
from jax.experimental import pallas as pl
from jax.experimental.pallas import tpu as pltpu

F32 = jnp.float32
BF16 = jnp.bfloat16
HIGHEST = lax.Precision.HIGHEST

EPS = 1e-6
CHUNK = 128
GRID_W = 64
CONV_K = 5
N_DIR = 2
SSD_HEAD_DIM = 64
SSD_STATE = 128
SSD_GROUPS = 2
GDN_HEAD_DIM = 128
LANE = 128
SUBLANE = 8
NORM_UNROLL = 8
SCAN_OUT_DTYPE = jnp.bfloat16
SSD_CHUNKS_PER_STEP = 2
GDN_CHUNKS_PER_STEP = 2
SUB_ROWS = 256
NORM_ROWS = 16
VMEM_LIMIT = 52 * 1024 * 1024

NT = (((1,), (1,)), ((), ()))
TN = (((0,), (0,)), ((), ()))


def _dot(a, b):
    return jnp.dot(a, b, preferred_element_type=F32)


def _dot_nt(a, b):
    return lax.dot_general(a, b, NT, preferred_element_type=F32)


def _dot_tn(a, b):
    return lax.dot_general(a, b, TN, preferred_element_type=F32)


def _sigmoid(x):
    return 1.0 / (1.0 + jnp.exp(-x))


def _silu(x):
    half = 0.5 * x
    return half + half * jnp.tanh(half)


def _softplus(x):
    return jnp.maximum(x, 0.0) + jnp.log(1.0 + jnp.exp(-jnp.abs(x)))


def _params(*sem, flags=None):
    return pltpu.CompilerParams(dimension_semantics=sem, vmem_limit_bytes=VMEM_LIMIT, flags=flags)


def _mod_kernel(c_ref, w_ref, b_ref, o_ref):
    a = _silu(c_ref[...])
    o_ref[0] = jnp.dot(a, w_ref[0], precision=HIGHEST, preferred_element_type=F32) + b_ref[0]


def _modulation(c_all, w_ada, b_ada):
    depth, d, n = w_ada.shape
    rows = c_all.shape[0]
    tn = 768
    return pl.pallas_call(
        _mod_kernel,
        name="adaln_mod",
        grid=(depth, n // tn),
        in_specs=[pl.BlockSpec((rows, d), lambda l, j: (0, 0)),
                  pl.BlockSpec((1, d, tn), lambda l, j: (l, 0, j)),
                  pl.BlockSpec((1, 1, tn), lambda l, j: (l, 0, j))],
        out_specs=pl.BlockSpec((1, rows, tn), lambda l, j: (l, 0, j)),
        out_shape=jax.ShapeDtypeStruct((depth, rows, n), F32),
        compiler_params=_params("parallel", "parallel"),
    )(c_all, w_ada, b_ada.reshape(depth, 1, n))


def _inproj_kernel(h_ref, shift_ref, scale_ref, prew_ref, w_ref, cw_ref, cb_ref, o_ref, u_ref,
                   *, group, j_lo, j_hi):
    j = pl.program_id(1)
    tm, tn = o_ref.shape
    sub = max(group, min(tm, SUB_ROWS))
    is_conv = (j >= j_lo) & (j < j_hi)

    @pl.when(j == 0)
    def _():
        prew = prew_ref[...]
        scale1p = 1.0 + scale_ref[0, 0]
        shift = shift_ref[0, 0]

        def norm_strip(s, carry):
            r0 = pl.multiple_of(s * NORM_ROWS, NORM_ROWS)
            x = h_ref[pl.ds(r0, NORM_ROWS), :]
            y = x * lax.rsqrt(jnp.mean(x * x, axis=-1, keepdims=True) + EPS) * prew
            u_ref[pl.ds(r0, NORM_ROWS), :] = (y * scale1p + shift).astype(BF16)
            return carry

        lax.fori_loop(0, tm // NORM_ROWS, norm_strip, 0, unroll=NORM_UNROLL)

    @pl.when(jnp.logical_not(is_conv))
    def _():
        o_ref[...] = _dot(u_ref[...], w_ref[...])

    @pl.when(is_conv)
    def _():
        cw = cw_ref[...]
        row = lax.broadcasted_iota(jnp.int32, (SUBLANE, tn), 0)
        offs = [k - CONV_K // 2 for k in range(CONV_K)]
        w_mid = [cw[k:k + 1] for k in range(CONV_K)]
        w_first = [jnp.where(row >= -off, w, 0.0) if off < 0 else w for off, w in zip(offs, w_mid)]
        w_last = [jnp.where(row < SUBLANE - off, w, 0.0) if off > 0 else w for off, w in zip(offs, w_mid)]
        bias = cb_ref[...]
        last0 = group - SUBLANE

        def taps(pieces, weights, lo, hi):
            out = bias
            for x, w in zip(pieces, weights):
                out = out + x[lo:hi] * w
            return _silu(out)

        n_sub = tm // sub
        acc_next = _dot(u_ref[0:sub, :], w_ref[...])
        for sb in range(n_sub):
            acc = acc_next
            if sb + 1 < n_sub:
                acc_next = _dot(u_ref[(sb + 1) * sub:(sb + 2) * sub, :], w_ref[...])
            shifted = [acc if off == 0 else pltpu.roll(acc, (-off) % sub, axis=0) for off in offs]
            for gi in range(sub // group):
                g0 = gi * group
                o0 = sb * sub + g0
                o_ref[o0:o0 + SUBLANE, :] = taps(shifted, w_first, g0, g0 + SUBLANE)
                o_ref[o0 + SUBLANE:o0 + last0, :] = taps(shifted, w_mid, g0 + SUBLANE, g0 + last0)
                o_ref[o0 + last0:o0 + group, :] = taps(shifted, w_last, g0 + last0, g0 + group)


def _inproj(h2d, shift, scale, pre_w, w, cw, cb, *, rows_per_mod, group, tm, tn, conv_cols):
    m, d = h2d.shape
    wp = w.shape[1]
    tiles_per_mod = rows_per_mod // tm
    kern = functools.partial(_inproj_kernel, group=group, j_lo=conv_cols[0] // tn, j_hi=conv_cols[1] // tn)
    return pl.pallas_call(
        kern,
        name="inproj",
        grid=(m // tm, wp // tn),
        in_specs=[pl.BlockSpec((tm, d), lambda i, j: (i, 0)),
                  pl.BlockSpec((1, 1, 1, d), lambda i, j: (i // tiles_per_mod, 0, 0, 0)),
                  pl.BlockSpec((1, 1, 1, d), lambda i, j: (i // tiles_per_mod, 0, 0, 1)),
                  pl.BlockSpec((1, d), lambda i, j: (0, 0)),
                  pl.BlockSpec((d, tn), lambda i, j: (0, j)),
                  pl.BlockSpec((8, tn), lambda i, j: (0, j)),
                  pl.BlockSpec((1, tn), lambda i, j: (0, j))],
        out_specs=pl.BlockSpec((tm, tn), lambda i, j: (i, j)),
        out_shape=jax.ShapeDtypeStruct((m, wp), F32),
        scratch_shapes=[pltpu.VMEM((tm, d), BF16)],
        compiler_params=_params("parallel", "arbitrary"),
    )(h2d, shift, scale, pre_w, w, cw, cb)


def _direction_masks(d):
    ri = lax.broadcasted_iota(jnp.int32, (CHUNK, CHUNK), 0)
    ci = lax.broadcasted_iota(jnp.int32, (CHUNK, CHUNK), 1)
    diff = (ri - ci) * (1 - 2 * d)
    return diff >= 0, diff > 0, ri, ci


def _chunk_index(d, p, nc):
    return p + d * (nc - 1 - 2 * p)


def _ssd_kernel(xs_ref, bc_ref, sm_ref, par_ref, dsk_ref, eq_ref, h0_ref, y_ref, hT_ref):
    d = pl.program_id(1)
    p = pl.program_id(2)
    heads_per_group = xs_ref.shape[-1] // (SSD_GROUPS * SSD_HEAD_DIM)
    gw = heads_per_group * SSD_HEAD_DIM

    @pl.when(p == 0)
    def _():
        hT_ref[...] = h0_ref[...]

    incl, _, _, _ = _direction_masks(d)
    fwd = d == 0
    tri = incl.astype(F32)
    par = par_ref[0]
    skip = dsk_ref[...] * (1 - d).astype(F32)
    cps = xs_ref.shape[1] // CHUNK
    rows = [pl.multiple_of((s + d * (cps - 1 - 2 * s)) * CHUNK, CHUNK) for s in range(cps)]
    groups = range(SSD_GROUPS)
    ppg = gw // LANE
    pairs = range(SSD_GROUPS * ppg)
    left = lax.broadcasted_iota(jnp.int32, (CHUNK, LANE), 1) < SSD_HEAD_DIM

    cg, xp, yd, eacum_x, cdec, upd = [], [], [], [], [], []
    for s in range(cps):
        sm = sm_ref[0, pl.ds(rows[s], CHUNK), :]
        xs = xs_ref[0, pl.ds(rows[s], CHUNK), :]
        bc = bc_ref[0, pl.ds(rows[s], CHUNK), :]
        dt = _softplus(sm + par[0:1])
        loga = dt * par[1:2]
        acum = jnp.dot(tri, loga, precision=HIGHEST, preferred_element_type=F32)
        dt_t = dt.T
        hi = acum.astype(BF16)
        lo = (acum - hi.astype(F32)).astype(BF16)
        acum_q = _dot(hi, eq_ref[...]) + _dot(lo, eq_ref[...])
        acum_t = (hi.astype(F32) + lo.astype(F32)).T
        bg = [bc[:, g * SSD_STATE:(g + 1) * SSD_STATE].astype(BF16) for g in groups]
        cgs = [bc[:, (SSD_GROUPS + g) * SSD_STATE:(SSD_GROUPS + g + 1) * SSD_STATE].astype(BF16) for g in groups]
        cb = [_dot_nt(cgs[g], bg[g]) for g in groups]

        def intra(hh):
            seg = acum_q[:, hh * CHUNK:(hh + 1) * CHUNK] - acum_t[hh:hh + 1, :]
            return (cb[hh // heads_per_group] * jnp.where(incl, jnp.exp(seg), 0.0)
                    * dt_t[hh:hh + 1, :]).astype(BF16)

        xps = [xs[:, i * LANE:(i + 1) * LANE] for i in pairs]
        m_pair = [jnp.concatenate([intra(2 * i), intra(2 * i + 1)], axis=1) for i in pairs]
        x_bd = [jnp.concatenate([jnp.where(left, x, 0.0), jnp.where(left, 0.0, x)], axis=0).astype(BF16)
                for x in xps]
        yd.append([_dot(m_pair[i], x_bd[i]) for i in pairs])
        acum_x = [jnp.where(left, acum_q[:, 2 * i * CHUNK:2 * i * CHUNK + LANE],
                            acum_q[:, (2 * i + 1) * CHUNK:(2 * i + 1) * CHUNK + LANE]) for i in pairs]
        alast_x = [jnp.where(fwd, a[CHUNK - 1:CHUNK], a[0:1]) for a in acum_x]
        dt_x = [jnp.where(left, dt[:, 2 * i:2 * i + 1], dt[:, 2 * i + 1:2 * i + 2]) for i in pairs]
        xw = [(xps[i] * (dt_x[i] * jnp.exp(alast_x[i] - acum_x[i]))).astype(BF16) for i in pairs]
        upd.append([_dot_tn(bg[g], jnp.concatenate(xw[g * ppg:(g + 1) * ppg], axis=1)) for g in groups])
        cdec.append([jnp.exp(jnp.concatenate(alast_x[g * ppg:(g + 1) * ppg], axis=1)) for g in groups])
        eacum_x.append([jnp.exp(a) for a in acum_x])
        cg.append(cgs)
        xp.append(xps)

    h_t = [hT_ref[0, 0, g] for g in groups]
    for s in range(cps):
        yoff = [_dot(cg[s][g], h_t[g].astype(BF16)) for g in groups]
        h_t = [h_t[g] * cdec[s][g] + upd[s][g] for g in groups]
        for i in pairs:
            lo = i * LANE
            yo = yoff[i // ppg][:, (i % ppg) * LANE:(i % ppg + 1) * LANE]
            y_ref[0, 0, pl.ds(rows[s], CHUNK), lo:lo + LANE] = (
                yd[s][i] + yo * eacum_x[s][i] + xp[s][i] * skip[:, lo:lo + LANE]).astype(y_ref.dtype)
    for g in groups:
        hT_ref[0, 0, g] = h_t[g]


def _ssd_scan(p3d, par, dsk, e_q, h0, *, cols):
    bsz, length, _ = p3d.shape
    nc = length // CHUNK
    c_xs, c_bc, c_sm, w_xs, w_bc = cols
    cps = min(SSD_CHUNKS_PER_STEP, nc)
    nb = nc // cps
    rows = cps * CHUNK
    chunk = functools.partial(_chunk_index, nc=nb)
    state_shape = h0.shape
    return pl.pallas_call(
        _ssd_kernel,
        name="ssd_scan",
        grid=(bsz, N_DIR, nb),
        in_specs=[pl.BlockSpec((1, rows, w_xs), lambda b, d, p: (b, chunk(d, p), c_xs // w_xs)),
                  pl.BlockSpec((1, rows, w_bc), lambda b, d, p: (b, chunk(d, p), c_bc // w_bc)),
                  pl.BlockSpec((1, rows, LANE), lambda b, d, p: (b, chunk(d, p), c_sm // LANE + d)),
                  pl.BlockSpec((1, 8, LANE), lambda b, d, p: (d, 0, 0)),
                  pl.BlockSpec((1, w_xs), lambda b, d, p: (0, 0)),
                  pl.BlockSpec(e_q.shape, lambda b, d, p: (0, 0)),
                  pl.BlockSpec((1, 1) + state_shape[2:], lambda b, d, p: (b, d, 0, 0, 0))],
        out_specs=[pl.BlockSpec((1, 1, rows, w_xs), lambda b, d, p: (d, b, chunk(d, p), 0)),
                   pl.BlockSpec((1, 1) + state_shape[2:], lambda b, d, p: (b, d, 0, 0, 0))],
        out_shape=[jax.ShapeDtypeStruct((N_DIR, bsz, length, w_xs), SCAN_OUT_DTYPE),
                   jax.ShapeDtypeStruct(state_shape, F32)],
        compiler_params=_params("parallel", "parallel", "arbitrary"),
    )(p3d, p3d, p3d, par, dsk, e_q, h0)


def _unit_triangular_inverse(a_list, xor_ij):
    eye = jnp.where(xor_ij == 0, 1.0, 0.0)
    in_pair = xor_ij < 2
    t_list = [eye - jnp.where(in_pair, a, 0.0) for a in a_list]
    size = 2
    while size < CHUNK:
        couples = (xor_ij >> (size.bit_length() - 1)) == 1
        a_off = [jnp.where(couples, a, 0.0).astype(BF16) for a in a_list]
        t16 = [t.astype(BF16) for t in t_list]
        ta = [_dot(t, a).astype(BF16) for t, a in zip(t16, a_off)]
        t_list = [t - _dot(x, tb) for t, x, tb in zip(t_list, ta, t16)]
        size *= 2
    return t_list


def _gdn_kernel(q_ref, k_ref, v_ref, sm_ref, par_ref, s0_ref, o_ref, s_ref, *, beta_lane, g_lane):
    d = pl.program_id(1)
    p = pl.program_id(2)
    heads = range(q_ref.shape[-1] // GDN_HEAD_DIM)

    @pl.when(p == 0)
    def _():
        s_ref[...] = s0_ref[...]

    incl, strict, ri, ci = _direction_masks(d)
    xor_ij = ri ^ ci
    fwd = d == 0
    tri = incl.astype(F32)
    par = par_ref[0]
    kk = GDN_HEAD_DIM
    cps = q_ref.shape[1] // CHUNK
    rows = [pl.multiple_of((s + d * (cps - 1 - 2 * s)) * CHUNK, CHUNK) for s in range(cps)]
    units = [(s, h) for s in range(cps) for h in heads]

    def col(x, lane):
        return x[:, lane:lane + 1]

    def head_cols(ref, s, h):
        return ref[0, pl.ds(rows[s], CHUNK), h * kk:(h + 1) * kk]

    eg, eend, cdec, beta_all, gcum, gcum_t = [], [], [], [], [], []
    for s in range(cps):
        sm = sm_ref[0, pl.ds(rows[s], CHUNK), :]
        beta_all.append(_sigmoid(sm))
        g_all = par[1:2] * _softplus(sm + par[0:1])
        gc = jnp.dot(tri, g_all, precision=HIGHEST, preferred_element_type=F32)
        glast = jnp.where(fwd, gc[CHUNK - 1:CHUNK], gc[0:1])
        gcum.append(gc)
        gcum_t.append(gc.T)
        eg.append(jnp.exp(gc))
        eend.append(jnp.exp(glast - gc))
        cdec.append(jnp.exp(glast))
    qn, kn = {}, {}
    for s, h in units:
        q = head_cols(q_ref, s, h)
        k = head_cols(k_ref, s, h)
        qn[s, h] = q * (lax.rsqrt(jnp.sum(q * q, axis=-1, keepdims=True) + EPS) * kk ** -0.5)
        kn[s, h] = k * lax.rsqrt(jnp.sum(k * k, axis=-1, keepdims=True) + EPS)
    beta = {(s, h): col(beta_all[s], beta_lane + h) for s, h in units}
    dec = {(s, h): jnp.where(incl, jnp.exp(col(gcum[s], g_lane + h) - gcum_t[s][g_lane + h:g_lane + h + 1, :]), 0.0)
           for s, h in units}
    kb = {x: kn[x] * beta[x] for x in units}
    k16 = {x: kn[x].astype(BF16) for x in units}
    a = [jnp.where(strict, _dot_nt(kb[x].astype(BF16), k16[x]) * dec[x], 0.0) for x in units]
    attn = {x: (_dot_nt(qn[x].astype(BF16), k16[x]) * dec[x]).astype(BF16) for x in units}
    t = dict(zip(units, _unit_triangular_inverse(a, xor_ij)))
    sol = {(s, h): _dot(t[s, h].astype(BF16), jnp.concatenate(
        [head_cols(v_ref, s, h) * beta[s, h], kb[s, h] * col(eg[s], g_lane + h)], axis=1).astype(BF16))
           for s, h in units}
    wq = {(s, h): jnp.concatenate([sol[s, h][:, kk:].astype(BF16),
                                   (qn[s, h] * col(eg[s], g_lane + h)).astype(BF16)], axis=0) for s, h in units}
    ke = {(s, h): (kn[s, h] * col(eend[s], g_lane + h)).astype(BF16) for s, h in units}

    state = [s_ref[0, 0, h] for h in heads]
    for s in range(cps):
        ws = [_dot(wq[s, h], state[h].astype(BF16)) for h in heads]
        vnew = [(sol[s, h][:, :kk] - ws[h][:CHUNK]).astype(BF16) for h in heads]
        o = [ws[h][CHUNK:] + _dot(attn[s, h], vnew[h]) for h in heads]
        state = [state[h] * col(cdec[s], g_lane + h) + _dot_tn(ke[s, h], vnew[h]) for h in heads]
        for h in heads:
            o_ref[0, 0, pl.ds(rows[s], CHUNK), h * kk:(h + 1) * kk] = o[h].astype(o_ref.dtype)
    for h in heads:
        s_ref[0, 0, h] = state[h]


def _gdn_scan(p3d, par, s0, *, cols, beta_lane, g_lane):
    bsz, length, _ = p3d.shape
    nc = length // CHUNK
    c_q, c_sm, w_q = cols
    cps = min(GDN_CHUNKS_PER_STEP, nc)
    nb = nc // cps
    rows = cps * CHUNK
    chunk = functools.partial(_chunk_index, nc=nb)
    state_shape = s0.shape
    kern = functools.partial(_gdn_kernel, beta_lane=beta_lane, g_lane=g_lane)
    qkv_spec = [pl.BlockSpec((1, rows, w_q), functools.partial(
        lambda b, d, p, i: (b, chunk(d, p), c_q // w_q + i), i=i)) for i in range(3)]
    return pl.pallas_call(
        kern,
        name="gdn_scan",
        grid=(bsz, N_DIR, nb),
        in_specs=qkv_spec + [
            pl.BlockSpec((1, rows, LANE), lambda b, d, p: (b, chunk(d, p), c_sm // LANE + d)),
            pl.BlockSpec((1, 8, LANE), lambda b, d, p: (d, 0, 0)),
            pl.BlockSpec((1, 1) + state_shape[2:], lambda b, d, p: (b, d, 0, 0, 0))],
        out_specs=[pl.BlockSpec((1, 1, rows, w_q), lambda b, d, p: (d, b, chunk(d, p), 0)),
                   pl.BlockSpec((1, 1) + state_shape[2:], lambda b, d, p: (b, d, 0, 0, 0))],
        out_shape=[jax.ShapeDtypeStruct((N_DIR, bsz, length, w_q), SCAN_OUT_DTYPE),
                   jax.ShapeDtypeStruct(state_shape, F32)],
        compiler_params=_params("parallel", "parallel", "arbitrary"),
    )(p3d, p3d, p3d, p3d, par, s0)


def _outproj_kernel(y0_ref, y1_ref, o0_ref, o1_ref, z_ref, h_ref, gate_ref, w_ref,
                    snw_ref, gnw_ref, postw_ref, out_ref):
    zs = _silu(z_ref[...])
    ssd_w = y0_ref.shape[-1]
    gw = ssd_w // SSD_GROUPS
    ys = (y0_ref[0].astype(F32) + y1_ref[0].astype(F32)) * zs[:, :ssd_w]
    parts = []
    for g in range(SSD_GROUPS):
        yg = ys[:, g * gw:(g + 1) * gw]
        parts.append(yg * lax.rsqrt(jnp.mean(yg * yg, axis=-1, keepdims=True) + EPS))
    ysn = jnp.concatenate(parts, axis=1) * snw_ref[...]
    og = o0_ref[0].astype(F32) + o1_ref[0].astype(F32)
    parts = []
    for h in range(og.shape[-1] // GDN_HEAD_DIM):
        oh = og[:, h * GDN_HEAD_DIM:(h + 1) * GDN_HEAD_DIM]
        parts.append(oh * lax.rsqrt(jnp.mean(oh * oh, axis=-1, keepdims=True) + EPS))
    ogn = jnp.concatenate(parts, axis=1) * gnw_ref[...] * zs[:, ssd_w:]
    ycat = jnp.concatenate([ysn, ogn], axis=1).astype(BF16)
    m = _dot(ycat, w_ref[...])
    mn = m * lax.rsqrt(jnp.mean(m * m, axis=-1, keepdims=True) + EPS) * postw_ref[...]
    out_ref[...] = h_ref[...] + gate_ref[0, 0] * mn


def _outproj(y, o, p2d, h2d, gate, w, snw, gnw, postw, *, rows_per_mod, tm):
    m, d = h2d.shape
    ssd_w = y.shape[-1]
    gdn_w = o.shape[-1]
    tiles_per_mod = rows_per_mod // tm
    y2 = y.reshape(N_DIR, m, ssd_w)
    o2 = o.reshape(N_DIR, m, gdn_w)
    return pl.pallas_call(
        _outproj_kernel,
        name="outproj",
        grid=(m // tm,),
        in_specs=[pl.BlockSpec((1, tm, ssd_w), lambda i: (0, i, 0)),
                  pl.BlockSpec((1, tm, ssd_w), lambda i: (1, i, 0)),
                  pl.BlockSpec((1, tm, gdn_w), lambda i: (0, i, 0)),
                  pl.BlockSpec((1, tm, gdn_w), lambda i: (1, i, 0)),
                  pl.BlockSpec((tm, ssd_w + gdn_w), lambda i: (i, 0)),
                  pl.BlockSpec((tm, d), lambda i: (i, 0)),
                  pl.BlockSpec((1, 1, 1, d), lambda i: (i // tiles_per_mod, 0, 0, 2)),
                  pl.BlockSpec((ssd_w + gdn_w, d), lambda i: (0, 0)),
                  pl.BlockSpec((1, ssd_w), lambda i: (0, 0)),
                  pl.BlockSpec((1, gdn_w), lambda i: (0, 0)),
                  pl.BlockSpec((1, d), lambda i: (0, 0))],
        out_specs=pl.BlockSpec((tm, d), lambda i: (i, 0)),
        out_shape=jax.ShapeDtypeStruct((m, d), F32),
        compiler_params=_params("parallel"),
    )(y2, y2, o2, o2, p2d, h2d, gate, w, snw, gnw, postw)


def kernel(x, c, ctx, c_ctx, w_ada, b_ada, pre_norm_w, post_norm_w, w_in, conv_ssd_w, conv_ssd_b,
           conv_gdn_w, ssd_a_log, ssd_dt_bias, ssd_d, ssd_norm_w, gdn_a_log, gdn_dt_bias,
           gdn_norm_w, w_out):
    bsz, seq, d = x.shape
    ctx_len = ctx.shape[1]
    depth = w_in.shape[0]
    d_mix = w_out.shape[1]
    ssd_w = d_mix // 2
    gdn_w = d_mix - ssd_w
    ssd_heads = ssd_w // SSD_HEAD_DIM
    gdn_heads = gdn_w // GDN_HEAD_DIM
    bc_w = 2 * SSD_GROUPS * SSD_STATE
    assert ssd_w == gdn_w and ssd_heads + 2 * gdn_heads <= LANE
    assert seq % CHUNK == 0 and ctx_len % CHUNK == 0 and CHUNK % GRID_W == 0

    c_z, c_xs = 0, d_mix
    c_qkv = c_xs + ssd_w
    c_bc = c_qkv + 3 * gdn_w
    c_sm = c_bc + bc_w
    tn = 512
    wp = -(-(c_sm + N_DIR * LANE) // tn) * tn
    assert c_qkv % (3 * gdn_w) == 0 or c_qkv % gdn_w == 0
    assert c_xs % ssd_w == 0 and c_bc % bc_w == 0 and c_xs % tn == 0 and c_sm % tn == 0
    beta_lane, g_lane = ssd_heads, ssd_heads + gdn_heads

    o_xbc = d_mix
    o_dt = o_xbc + ssd_w + bc_w
    o_qkv = o_dt + N_DIR * ssd_heads
    o_beta = o_qkv + 3 * gdn_w
    o_alpha = o_beta + N_DIR * gdn_heads

    def small_cols(dd):
        parts = [w_in[:, :, o_dt + dd * ssd_heads:o_dt + (dd + 1) * ssd_heads],
                 w_in[:, :, o_beta + dd * gdn_heads:o_beta + (dd + 1) * gdn_heads],
                 w_in[:, :, o_alpha + dd * gdn_heads:o_alpha + (dd + 1) * gdn_heads]]
        used = ssd_heads + 2 * gdn_heads
        return parts + [jnp.zeros((depth, d, LANE - used), w_in.dtype)]

    w_cat = jnp.concatenate(
        [w_in[:, :, :d_mix], w_in[:, :, o_xbc:o_xbc + ssd_w], w_in[:, :, o_qkv:o_qkv + 3 * gdn_w],
         w_in[:, :, o_xbc + ssd_w:o_xbc + ssd_w + bc_w]] + small_cols(0) + small_cols(1)
        + [jnp.zeros((depth, d, wp - c_sm - N_DIR * LANE), w_in.dtype)], axis=-1).astype(BF16)

    def conv_layout(ssd_part, gdn_part):
        lead = ssd_part.shape[:-1]
        return jnp.concatenate(
            [jnp.zeros(lead + (d_mix,), F32), ssd_part[..., :ssd_w], gdn_part, ssd_part[..., ssd_w:],
             jnp.zeros(lead + (wp - c_sm,), F32)], axis=-1)

    cw_all = conv_layout(conv_ssd_w.astype(F32), conv_gdn_w.astype(F32))
    cw_all = jnp.concatenate([cw_all, jnp.zeros((depth, 8 - CONV_K, wp), F32)], axis=1)
    cb_all = conv_layout(conv_ssd_b.astype(F32)[:, None], jnp.zeros((depth, 1, 3 * gdn_w), F32))

    def lane_row(vals, lane0):
        return jnp.pad(vals, ((0, 0), (0, 0), (lane0, LANE - lane0 - vals.shape[-1])))

    zeros_rows = jnp.zeros((depth, N_DIR, 6, LANE), F32)
    ssd_par = jnp.concatenate([lane_row(ssd_dt_bias.astype(F32), 0)[:, :, None],
                               lane_row(-jnp.exp(ssd_a_log.astype(F32)), 0)[:, :, None], zeros_rows], axis=2)
    gdn_par = jnp.concatenate([lane_row(gdn_dt_bias.astype(F32), g_lane)[:, :, None],
                               lane_row(-jnp.exp(gdn_a_log.astype(F32)), g_lane)[:, :, None], zeros_rows], axis=2)
    dsk_all = jnp.repeat(ssd_d.astype(F32), SSD_HEAD_DIM, axis=-1)[:, None]
    gnw_all = jnp.tile(gdn_norm_w.astype(F32), (1, gdn_heads))[:, None]
    w_out16 = w_out.astype(BF16)
    head_of_lane = jnp.arange(LANE)[:, None]
    e_q = (jnp.arange(ssd_heads * CHUNK)[None] // CHUNK == head_of_lane).astype(BF16)

    rows = 16
    c_all = jnp.concatenate([c, c_ctx[None], jnp.zeros((rows - bsz - 1, d), c.dtype)], axis=0)
    mod = _modulation(c_all, w_ada, b_ada).reshape(depth, rows, 1, 3 * d)

    h_lat = x.reshape(bsz * seq, d)
    h_ctx = ctx.reshape(bsz * ctx_len, d)
    tm_lat = min(1024, seq)
    tm_ctx = min(1024, bsz * ctx_len)
    tm_out = 256
    conv_cols = (c_xs, c_sm)
    ssd_cols = (c_xs, c_bc, c_sm, ssd_w, bc_w)
    gdn_cols = (c_qkv, c_sm, gdn_w)
    ssd_state0 = jnp.zeros((bsz, N_DIR, SSD_GROUPS, SSD_STATE, ssd_w // SSD_GROUPS), F32)
    gdn_state0 = jnp.zeros((bsz, N_DIR, gdn_heads, GDN_HEAD_DIM, GDN_HEAD_DIM), F32)

    for l in range(depth):
        mod_lat = mod[l, :bsz, None]
        mod_ctx = mod[l, bsz:bsz + 1, None]
        pre_w = pre_norm_w[l][None]
        p_lat = _inproj(h_lat, mod_lat, mod_lat, pre_w, w_cat[l], cw_all[l], cb_all[l],
                        rows_per_mod=seq, group=GRID_W, tm=tm_lat, tn=tn, conv_cols=conv_cols)
        p_ctx = _inproj(h_ctx, mod_ctx, mod_ctx, pre_w, w_cat[l], cw_all[l], cb_all[l],
                        rows_per_mod=bsz * ctx_len, group=ctx_len, tm=tm_ctx, tn=tn, conv_cols=conv_cols)
        p_lat3 = p_lat.reshape(bsz, seq, wp)
        p_ctx3 = p_ctx.reshape(bsz, ctx_len, wp)

        y_ctx, ssd_state = _ssd_scan(p_ctx3, ssd_par[l], dsk_all[l], e_q,ssd_state0, cols=ssd_cols)
        y_lat, _ = _ssd_scan(p_lat3, ssd_par[l], dsk_all[l], e_q,ssd_state, cols=ssd_cols)
        o_ctx, gdn_state = _gdn_scan(p_ctx3, gdn_par[l], gdn_state0, cols=gdn_cols,
                                     beta_lane=beta_lane, g_lane=g_lane)
        o_lat, _ = _gdn_scan(p_lat3, gdn_par[l], gdn_state, cols=gdn_cols,
                             beta_lane=beta_lane, g_lane=g_lane)

        snw = ssd_norm_w[l][None].astype(F32)
        post_w = post_norm_w[l][None]
        h_lat_new = _outproj(y_lat, o_lat, p_lat, h_lat, mod_lat, w_out16[l], snw, gnw_all[l], post_w,
                             rows_per_mod=seq, tm=min(tm_out, seq))
        if l < depth - 1:
            h_ctx = _outproj(y_ctx, o_ctx, p_ctx, h_ctx, mod_ctx, w_out16[l], snw, gnw_all[l], post_w,
                             rows_per_mod=bsz * ctx_len, tm=min(tm_out, bsz * ctx_len))
        h_lat = h_lat_new
    return h_lat.reshape(bsz, seq, d)
```

```python
import functools

import jax
import jax.numpy as jnp
from jax import lax
from jax.experimental import pallas as pl
from jax.experimental.pallas import tpu as pltpu

F32 = jnp.float32
BF16 = jnp.bfloat16
HIGHEST = lax.Precision.HIGHEST

EPS = 1e-6
CHUNK = 128
GRID_W = 64
CONV_K = 5
N_DIR = 2
SSD_HEAD_DIM = 64
SSD_STATE = 128
SSD_GROUPS = 2
GDN_HEAD_DIM = 128
LANE = 128
SUBLANE = 8
NORM_UNROLL = 8
SCAN_OUT_DTYPE = jnp.bfloat16
SSD_CHUNKS_PER_STEP = 2
GDN_CHUNKS_PER_STEP = 2
SUB_ROWS = 256
NORM_ROWS = 16
VMEM_LIMIT = 52 * 1024 * 1024

NT = (((1,), (1,)), ((), ()))
TN = (((0,), (0,)), ((), ()))


def _dot(a, b):
    return jnp.dot(a, b, preferred_element_type=F32)


def _dot_nt(a, b):
    return lax.dot_general(a, b, NT, preferred_element_type=F32)


def _dot_tn(a, b):
    return lax.dot_general(a, b, TN, preferred_element_type=F32)


def _sigmoid(x):
    return 1.0 / (1.0 + jnp.exp(-x))


def _silu(x):
    half = 0.5 * x
    return half + half * jnp.tanh(half)


def _softplus(x):
    return jnp.maximum(x, 0.0) + jnp.log(1.0 + jnp.exp(-jnp.abs(x)))


def _params(*sem, flags=None):
    return pltpu.CompilerParams(dimension_semantics=sem, vmem_limit_bytes=VMEM_LIMIT, flags=flags)


def _mod_kernel(c_ref, w_ref, b_ref, o_ref):
    a = _silu(c_ref[...])
    o_ref[0] = jnp.dot(a, w_ref[0], precision=HIGHEST, preferred_element_type=F32) + b_ref[0]


def _modulation(c_all, w_ada, b_ada):
    depth, d, n = w_ada.shape
    rows = c_all.shape[0]
    tn = 768
    return pl.pallas_call(
        _mod_kernel,
        name="adaln_mod",
        grid=(depth, n // tn),
        in_specs=[pl.BlockSpec((rows, d), lambda l, j: (0, 0)),
                  pl.BlockSpec((1, d, tn), lambda l, j: (l, 0, j)),
                  pl.BlockSpec((1, 1, tn), lambda l, j: (l, 0, j))],
        out_specs=pl.BlockSpec((1, rows, tn), lambda l, j: (l, 0, j)),
        out_shape=jax.ShapeDtypeStruct((depth, rows, n), F32),
        compiler_params=_params("parallel", "parallel"),
    )(c_all, w_ada, b_ada.reshape(depth, 1, n))


def _inproj_kernel(h_ref, shift_ref, scale_ref, prew_ref, w_ref, cw_ref, cb_ref, o_ref, u_ref,
                   *, group, j_lo, j_hi, mixed_lanes):
    j = pl.program_id(1)
    tm, tn = o_ref.shape
    sub = max(group, min(tm, SUB_ROWS))

    @pl.when(j == 0)
    def _():
        prew = prew_ref[...]
        scale1p = 1.0 + scale_ref[0, 0]
        shift = shift_ref[0, 0]

        def norm_strip(s, carry):
            r0 = pl.multiple_of(s * NORM_ROWS, NORM_ROWS)
            x = h_ref[pl.ds(r0, NORM_ROWS), :]
            y = x * lax.rsqrt(jnp.mean(x * x, axis=-1, keepdims=True) + EPS) * prew
            u_ref[pl.ds(r0, NORM_ROWS), :] = (y * scale1p + shift).astype(BF16)
            return carry

        lax.fori_loop(0, tm // NORM_ROWS, norm_strip, 0, unroll=NORM_UNROLL)

    def conv_tile(lanes):
        cw = cw_ref[:, :lanes]
        bias = cb_ref[:, :lanes]
        row = lax.broadcasted_iota(jnp.int32, (SUBLANE, lanes), 0)
        offs = [k - CONV_K // 2 for k in range(CONV_K)]
        w_mid = [cw[k:k + 1] for k in range(CONV_K)]
        w_first = [jnp.where(row >= -off, w, 0.0) if off < 0 else w for off, w in zip(offs, w_mid)]
        w_last = [jnp.where(row < SUBLANE - off, w, 0.0) if off > 0 else w for off, w in zip(offs, w_mid)]
        last0 = group - SUBLANE

        def taps(pieces, weights, lo, hi):
            out = bias
            for x, w in zip(pieces, weights):
                out = out + x[lo:hi] * w
            return _silu(out)

        n_sub = tm // sub
        acc_next = _dot(u_ref[0:sub, :], w_ref[...])
        for sb in range(n_sub):
            acc = acc_next
            if sb + 1 < n_sub:
                acc_next = _dot(u_ref[(sb + 1) * sub:(sb + 2) * sub, :], w_ref[...])
            if lanes < tn:
                o_ref[sb * sub:(sb + 1) * sub, lanes:] = acc[:, lanes:]
            a = acc[:, :lanes]
            shifted = [a if off == 0 else pltpu.roll(a, (-off) % sub, axis=0) for off in offs]
            for gi in range(sub // group):
                g0 = gi * group
                o0 = sb * sub + g0
                o_ref[o0:o0 + SUBLANE, :lanes] = taps(shifted, w_first, g0, g0 + SUBLANE)
                o_ref[o0 + SUBLANE:o0 + last0, :lanes] = taps(shifted, w_mid, g0 + SUBLANE, g0 + last0)
                o_ref[o0 + last0:o0 + group, :lanes] = taps(shifted, w_last, g0 + last0, g0 + group)

    @pl.when((j < j_lo) if mixed_lanes else ((j < j_lo) | (j >= j_hi)))
    def _():
        o_ref[...] = _dot(u_ref[...], w_ref[...])

    @pl.when((j >= j_lo) & (j < j_hi))
    def _():
        conv_tile(tn)

    if mixed_lanes:
        @pl.when(j >= j_hi)
        def _():
            conv_tile(mixed_lanes)


def _inproj(h2d, shift, scale, pre_w, w_all, layer, cw, cb, *, rows_per_mod, group, tm, tn, conv_cols):
    m, d = h2d.shape
    wp = w_all.shape[-1]
    tiles_per_mod = rows_per_mod // tm
    kern = functools.partial(_inproj_kernel, group=group, j_lo=conv_cols[0] // tn, j_hi=conv_cols[1] // tn,
                             mixed_lanes=conv_cols[1] % tn)
    return pl.pallas_call(
        kern,
        name="inproj",
        grid=(m // tm, wp // tn),
        in_specs=[pl.BlockSpec((tm, d), lambda i, j: (i, 0)),
                  pl.BlockSpec((1, 1, 1, d), lambda i, j: (i // tiles_per_mod, 0, 0, 0)),
                  pl.BlockSpec((1, 1, 1, d), lambda i, j: (i // tiles_per_mod, 0, 0, 1)),
                  pl.BlockSpec((1, d), lambda i, j: (0, 0)),
                  pl.BlockSpec((None, d, tn), lambda i, j: (layer, 0, j)),
                  pl.BlockSpec((8, tn), lambda i, j: (0, j)),
                  pl.BlockSpec((1, tn), lambda i, j: (0, j))],
        out_specs=pl.BlockSpec((tm, tn), lambda i, j: (i, j)),
        out_shape=jax.ShapeDtypeStruct((m, wp), F32),
        scratch_shapes=[pltpu.VMEM((tm, d), BF16)],
        compiler_params=_params("parallel", "arbitrary"),
    )(h2d, shift, scale, pre_w, w_all, cw, cb)


def _direction_masks(d):
    ri = lax.broadcasted_iota(jnp.int32, (CHUNK, CHUNK), 0)
    ci = lax.broadcasted_iota(jnp.int32, (CHUNK, CHUNK), 1)
    diff = (ri - ci) * (1 - 2 * d)
    return diff >= 0, diff > 0, ri, ci


def _chunk_index(d, p, nc):
    return p + d * (nc - 1 - 2 * p)


def _ssd_kernel(xs_ref, bc_ref, sm_ref, par_ref, dsk_ref, eq_ref, h0_ref, y_ref, hT_ref):
    d = pl.program_id(1)
    p = pl.program_id(2)
    heads_per_group = xs_ref.shape[-1] // (SSD_GROUPS * SSD_HEAD_DIM)
    gw = heads_per_group * SSD_HEAD_DIM

    @pl.when(p == 0)
    def _():
        hT_ref[...] = h0_ref[...]

    incl, _, _, _ = _direction_masks(d)
    fwd = d == 0
    tri = incl.astype(F32)
    par = par_ref[0]
    skip = dsk_ref[...] * (1 - d).astype(F32)
    cps = xs_ref.shape[1] // CHUNK
    rows = [pl.multiple_of((s + d * (cps - 1 - 2 * s)) * CHUNK, CHUNK) for s in range(cps)]
    groups = range(SSD_GROUPS)
    ppg = gw // LANE
    pairs = range(SSD_GROUPS * ppg)
    left = lax.broadcasted_iota(jnp.int32, (CHUNK, LANE), 1) < SSD_HEAD_DIM

    cg, xp, yd, eacum_x, cdec, upd = [], [], [], [], [], []
    for s in range(cps):
        sm = sm_ref[0, pl.ds(rows[s], CHUNK), :]
        xs = xs_ref[0, pl.ds(rows[s], CHUNK), :]
        bc = bc_ref[0, pl.ds(rows[s], CHUNK), :]
        dt = _softplus(sm + par[0:1])
        loga = dt * par[1:2]
        acum = jnp.dot(tri, loga, precision=HIGHEST, preferred_element_type=F32)
        dt_t = dt.T
        hi = acum.astype(BF16)
        lo = (acum - hi.astype(F32)).astype(BF16)
        acum_q = _dot(hi, eq_ref[...]) + _dot(lo, eq_ref[...])
        acum_t = (hi.astype(F32) + lo.astype(F32)).T
        bg = [bc[:, g * SSD_STATE:(g + 1) * SSD_STATE].astype(BF16) for g in groups]
        cgs = [bc[:, (SSD_GROUPS + g) * SSD_STATE:(SSD_GROUPS + g + 1) * SSD_STATE].astype(BF16) for g in groups]
        cb = [_dot_nt(cgs[g], bg[g]) for g in groups]

        def intra(hh):
            seg = acum_q[:, hh * CHUNK:(hh + 1) * CHUNK] - acum_t[hh:hh + 1, :]
            return (cb[hh // heads_per_group] * jnp.where(incl, jnp.exp(seg), 0.0)
                    * dt_t[hh:hh + 1, :]).astype(BF16)

        xps = [xs[:, i * LANE:(i + 1) * LANE] for i in pairs]
        m_pair = [jnp.concatenate([intra(2 * i), intra(2 * i + 1)], axis=1) for i in pairs]
        x_bd = [jnp.concatenate([jnp.where(left, x, 0.0), jnp.where(left, 0.0, x)], axis=0).astype(BF16)
                for x in xps]
        yd.append([_dot(m_pair[i], x_bd[i]) for i in pairs])
        acum_x = [jnp.where(left, acum_q[:, 2 * i * CHUNK:2 * i * CHUNK + LANE],
                            acum_q[:, (2 * i + 1) * CHUNK:(2 * i + 1) * CHUNK + LANE]) for i in pairs]
        alast_x = [jnp.where(fwd, a[CHUNK - 1:CHUNK], a[0:1]) for a in acum_x]
        dt_x = [jnp.where(left, dt[:, 2 * i:2 * i + 1], dt[:, 2 * i + 1:2 * i + 2]) for i in pairs]
        xw = [(xps[i] * (dt_x[i] * jnp.exp(alast_x[i] - acum_x[i]))).astype(BF16) for i in pairs]
        upd.append([_dot_tn(bg[g], jnp.concatenate(xw[g * ppg:(g + 1) * ppg], axis=1)) for g in groups])
        cdec.append([jnp.exp(jnp.concatenate(alast_x[g * ppg:(g + 1) * ppg], axis=1)) for g in groups])
        eacum_x.append([jnp.exp(a) for a in acum_x])
        cg.append(cgs)
        xp.append(xps)

    h_t = [hT_ref[0, 0, g] for g in groups]
    for s in range(cps):
        yoff = [_dot(cg[s][g], h_t[g].astype(BF16)) for g in groups]
        h_t = [h_t[g] * cdec[s][g] + upd[s][g] for g in groups]
        for i in pairs:
            lo = i * LANE
            yo = yoff[i // ppg][:, (i % ppg) * LANE:(i % ppg + 1) * LANE]
            y_ref[0, 0, pl.ds(rows[s], CHUNK), lo:lo + LANE] = (
                yd[s][i] + yo * eacum_x[s][i] + xp[s][i] * skip[:, lo:lo + LANE]).astype(y_ref.dtype)
    for g in groups:
        hT_ref[0, 0, g] = h_t[g]


def _ssd_scan(p3d, par, dsk, e_q, h0, *, cols):
    bsz, length, _ = p3d.shape
    nc = length // CHUNK
    c_xs, c_bc, c_sm, w_xs, w_bc = cols
    cps = min(SSD_CHUNKS_PER_STEP, nc)
    nb = nc // cps
    rows = cps * CHUNK
    chunk = functools.partial(_chunk_index, nc=nb)
    state_shape = h0.shape
    return pl.pallas_call(
        _ssd_kernel,
        name="ssd_scan",
        grid=(bsz, N_DIR, nb),
        in_specs=[pl.BlockSpec((1, rows, w_xs), lambda b, d, p: (b, chunk(d, p), c_xs // w_xs)),
                  pl.BlockSpec((1, rows, w_bc), lambda b, d, p: (b, chunk(d, p), c_bc // w_bc)),
                  pl.BlockSpec((1, rows, LANE), lambda b, d, p: (b, chunk(d, p), c_sm // LANE + d)),
                  pl.BlockSpec((1, 8, LANE), lambda b, d, p: (d, 0, 0)),
                  pl.BlockSpec((1, w_xs), lambda b, d, p: (0, 0)),
                  pl.BlockSpec(e_q.shape, lambda b, d, p: (0, 0)),
                  pl.BlockSpec((1, 1) + state_shape[2:], lambda b, d, p: (b, d, 0, 0, 0))],
        out_specs=[pl.BlockSpec((1, 1, rows, w_xs), lambda b, d, p: (d, b, chunk(d, p), 0)),
                   pl.BlockSpec((1, 1) + state_shape[2:], lambda b, d, p: (b, d, 0, 0, 0))],
        out_shape=[jax.ShapeDtypeStruct((N_DIR, bsz, length, w_xs), SCAN_OUT_DTYPE),
                   jax.ShapeDtypeStruct(state_shape, F32)],
        compiler_params=_params("parallel", "parallel", "arbitrary"),
    )(p3d, p3d, p3d, par, dsk, e_q, h0)


def _unit_triangular_inverse(a_list, xor_ij):
    eye = jnp.where(xor_ij == 0, 1.0, 0.0)
    in_pair = xor_ij < 2
    t_list = [eye - jnp.where(in_pair, a, 0.0) for a in a_list]
    size = 2
    while size < CHUNK:
        couples = (xor_ij >> (size.bit_length() - 1)) == 1
        a_off = [jnp.where(couples, a, 0.0).astype(BF16) for a in a_list]
        t16 = [t.astype(BF16) for t in t_list]
        ta = [_dot(t, a).astype(BF16) for t, a in zip(t16, a_off)]
        t_list = [t - _dot(x, tb) for t, x, tb in zip(t_list, ta, t16)]
        size *= 2
    return t_list


def _gdn_kernel(q_ref, k_ref, v_ref, sm_ref, par_ref, s0_ref, o_ref, s_ref, *, beta_lane, g_lane):
    d = pl.program_id(1)
    p = pl.program_id(2)
    heads = range(q_ref.shape[-1] // GDN_HEAD_DIM)

    @pl.when(p == 0)
    def _():
        s_ref[...] = s0_ref[...]

    incl, strict, ri, ci = _direction_masks(d)
    xor_ij = ri ^ ci
    fwd = d == 0
    tri = incl.astype(F32)
    par = par_ref[0]
    kk = GDN_HEAD_DIM
    cps = q_ref.shape[1] // CHUNK
    rows = [pl.multiple_of((s + d * (cps - 1 - 2 * s)) * CHUNK, CHUNK) for s in range(cps)]
    units = [(s, h) for s in range(cps) for h in heads]

    def col(x, lane):
        return x[:, lane:lane + 1]

    def head_cols(ref, s, h):
        return ref[0, pl.ds(rows[s], CHUNK), h * kk:(h + 1) * kk]

    eg, eend, cdec, beta_all, gcum, gcum_t = [], [], [], [], [], []
    for s in range(cps):
        sm = sm_ref[0, pl.ds(rows[s], CHUNK), :]
        beta_all.append(_sigmoid(sm))
        g_all = par[1:2] * _softplus(sm + par[0:1])
        gc = jnp.dot(tri, g_all, precision=HIGHEST, preferred_element_type=F32)
        glast = jnp.where(fwd, gc[CHUNK - 1:CHUNK], gc[0:1])
        gcum.append(gc)
        gcum_t.append(gc.T)
        eg.append(jnp.exp(gc))
        eend.append(jnp.exp(glast - gc))
        cdec.append(jnp.exp(glast))
    qn, kn = {}, {}
    for s, h in units:
        q = head_cols(q_ref, s, h)
        k = head_cols(k_ref, s, h)
        qn[s, h] = q * (lax.rsqrt(jnp.sum(q * q, axis=-1, keepdims=True) + EPS) * kk ** -0.5)
        kn[s, h] = k * lax.rsqrt(jnp.sum(k * k, axis=-1, keepdims=True) + EPS)
    beta = {(s, h): col(beta_all[s], beta_lane + h) for s, h in units}
    dec = {(s, h): jnp.where(incl, jnp.exp(col(gcum[s], g_lane + h) - gcum_t[s][g_lane + h:g_lane + h + 1, :]), 0.0)
           for s, h in units}
    kb = {x: kn[x] * beta[x] for x in units}
    k16 = {x: kn[x].astype(BF16) for x in units}
    a = [jnp.where(strict, _dot_nt(kb[x].astype(BF16), k16[x]) * dec[x], 0.0) for x in units]
    attn = {x: (_dot_nt(qn[x].astype(BF16), k16[x]) * dec[x]).astype(BF16) for x in units}
    t = dict(zip(units, _unit_triangular_inverse(a, xor_ij)))
    sol = {(s, h): _dot(t[s, h].astype(BF16), jnp.concatenate(
        [head_cols(v_ref, s, h) * beta[s, h], kb[s, h] * col(eg[s], g_lane + h)], axis=1).astype(BF16))
           for s, h in units}
    wq = {(s, h): jnp.concatenate([sol[s, h][:, kk:].astype(BF16),
                                   (qn[s, h] * col(eg[s], g_lane + h)).astype(BF16)], axis=0) for s, h in units}
    ke = {(s, h): (kn[s, h] * col(eend[s], g_lane + h)).astype(BF16) for s, h in units}

    state = [s_ref[0, 0, h] for h in heads]
    for s in range(cps):
        ws = [_dot(wq[s, h], state[h].astype(BF16)) for h in heads]
        vnew = [(sol[s, h][:, :kk] - ws[h][:CHUNK]).astype(BF16) for h in heads]
        o = [ws[h][CHUNK:] + _dot(attn[s, h], vnew[h]) for h in heads]
        state = [state[h] * col(cdec[s], g_lane + h) + _dot_tn(ke[s, h], vnew[h]) for h in heads]
        for h in heads:
            o_ref[0, 0, pl.ds(rows[s], CHUNK), h * kk:(h + 1) * kk] = o[h].astype(o_ref.dtype)
    for h in heads:
        s_ref[0, 0, h] = state[h]


def _gdn_scan(p3d, par, s0, *, cols, beta_lane, g_lane):
    bsz, length, _ = p3d.shape
    nc = length // CHUNK
    c_q, c_sm, w_q = cols
    cps = min(GDN_CHUNKS_PER_STEP, nc)
    nb = nc // cps
    rows = cps * CHUNK
    chunk = functools.partial(_chunk_index, nc=nb)
    state_shape = s0.shape
    kern = functools.partial(_gdn_kernel, beta_lane=beta_lane, g_lane=g_lane)
    qkv_spec = [pl.BlockSpec((1, rows, w_q), functools.partial(
        lambda b, d, p, i: (b, chunk(d, p), c_q // w_q + i), i=i)) for i in range(3)]
    return pl.pallas_call(
        kern,
        name="gdn_scan",
        grid=(bsz, N_DIR, nb),
        in_specs=qkv_spec + [
            pl.BlockSpec((1, rows, LANE), lambda b, d, p: (b, chunk(d, p), c_sm // LANE + d)),
            pl.BlockSpec((1, 8, LANE), lambda b, d, p: (d, 0, 0)),
            pl.BlockSpec((1, 1) + state_shape[2:], lambda b, d, p: (b, d, 0, 0, 0))],
        out_specs=[pl.BlockSpec((1, 1, rows, w_q), lambda b, d, p: (d, b, chunk(d, p), 0)),
                   pl.BlockSpec((1, 1) + state_shape[2:], lambda b, d, p: (b, d, 0, 0, 0))],
        out_shape=[jax.ShapeDtypeStruct((N_DIR, bsz, length, w_q), SCAN_OUT_DTYPE),
                   jax.ShapeDtypeStruct(state_shape, F32)],
        compiler_params=_params("parallel", "parallel", "arbitrary"),
    )(p3d, p3d, p3d, p3d, par, s0)


def _outproj_kernel(y0_ref, y1_ref, o0_ref, o1_ref, z_ref, h_ref, gate_ref, w_ref,
                    snw_ref, gnw_ref, postw_ref, out_ref):
    zs = _silu(z_ref[...])
    ssd_w = y0_ref.shape[-1]
    gw = ssd_w // SSD_GROUPS
    ys = (y0_ref[0].astype(F32) + y1_ref[0].astype(F32)) * zs[:, :ssd_w]
    parts = []
    for g in range(SSD_GROUPS):
        yg = ys[:, g * gw:(g + 1) * gw]
        parts.append(yg * lax.rsqrt(jnp.mean(yg * yg, axis=-1, keepdims=True) + EPS))
    ysn = jnp.concatenate(parts, axis=1) * snw_ref[...]
    og = o0_ref[0].astype(F32) + o1_ref[0].astype(F32)
    parts = []
    for h in range(og.shape[-1] // GDN_HEAD_DIM):
        oh = og[:, h * GDN_HEAD_DIM:(h + 1) * GDN_HEAD_DIM]
        parts.append(oh * lax.rsqrt(jnp.mean(oh * oh, axis=-1, keepdims=True) + EPS))
    ogn = jnp.concatenate(parts, axis=1) * gnw_ref[...] * zs[:, ssd_w:]
    ycat = jnp.concatenate([ysn, ogn], axis=1).astype(BF16)
    m = _dot(ycat, w_ref[...])
    mn = m * lax.rsqrt(jnp.mean(m * m, axis=-1, keepdims=True) + EPS) * postw_ref[...]
    out_ref[...] = h_ref[...] + gate_ref[0, 0] * mn


def _outproj(y, o, p2d, h2d, gate, w_all, layer, snw, gnw, postw, *, rows_per_mod, tm):
    m, d = h2d.shape
    ssd_w = y.shape[-1]
    gdn_w = o.shape[-1]
    tiles_per_mod = rows_per_mod // tm
    y2 = y.reshape(N_DIR, m, ssd_w)
    o2 = o.reshape(N_DIR, m, gdn_w)
    return pl.pallas_call(
        _outproj_kernel,
        name="outproj",
        grid=(m // tm,),
        in_specs=[pl.BlockSpec((1, tm, ssd_w), lambda i: (0, i, 0)),
                  pl.BlockSpec((1, tm, ssd_w), lambda i: (1, i, 0)),
                  pl.BlockSpec((1, tm, gdn_w), lambda i: (0, i, 0)),
                  pl.BlockSpec((1, tm, gdn_w), lambda i: (1, i, 0)),
                  pl.BlockSpec((tm, ssd_w + gdn_w), lambda i: (i, 0)),
                  pl.BlockSpec((tm, d), lambda i: (i, 0)),
                  pl.BlockSpec((1, 1, 1, d), lambda i: (i // tiles_per_mod, 0, 0, 2)),
                  pl.BlockSpec((None, ssd_w + gdn_w, d), lambda i: (layer, 0, 0)),
                  pl.BlockSpec((1, ssd_w), lambda i: (0, 0)),
                  pl.BlockSpec((1, gdn_w), lambda i: (0, 0)),
                  pl.BlockSpec((1, d), lambda i: (0, 0))],
        out_specs=pl.BlockSpec((tm, d), lambda i: (i, 0)),
        out_shape=jax.ShapeDtypeStruct((m, d), F32),
        compiler_params=_params("parallel"),
    )(y2, y2, o2, o2, p2d, h2d, gate, w_all, snw, gnw, postw)


def kernel(x, c, ctx, c_ctx, w_ada, b_ada, pre_norm_w, post_norm_w, w_in, conv_ssd_w, conv_ssd_b,
           conv_gdn_w, ssd_a_log, ssd_dt_bias, ssd_d, ssd_norm_w, gdn_a_log, gdn_dt_bias,
           gdn_norm_w, w_out):
    bsz, seq, d = x.shape
    ctx_len = ctx.shape[1]
    depth = w_in.shape[0]
    d_mix = w_out.shape[1]
    ssd_w = d_mix // 2
    gdn_w = d_mix - ssd_w
    ssd_heads = ssd_w // SSD_HEAD_DIM
    gdn_heads = gdn_w // GDN_HEAD_DIM
    bc_w = 2 * SSD_GROUPS * SSD_STATE
    assert ssd_w == gdn_w and ssd_heads + 2 * gdn_heads <= LANE
    assert seq % CHUNK == 0 and ctx_len % CHUNK == 0 and CHUNK % GRID_W == 0

    c_z, c_xs = 0, d_mix
    c_qkv = c_xs + ssd_w
    c_bc = c_qkv + 3 * gdn_w
    c_sm = c_bc + bc_w
    tn = 1024
    wp = -(-(c_sm + N_DIR * LANE) // tn) * tn
    assert c_qkv % gdn_w == 0 and c_xs % ssd_w == 0 and c_bc % bc_w == 0 and c_sm % LANE == 0
    assert c_xs % tn == 0 and c_sm // tn == wp // tn - 1
    beta_lane, g_lane = ssd_heads, ssd_heads + gdn_heads

    o_xbc = d_mix
    o_dt = o_xbc + ssd_w + bc_w
    o_qkv = o_dt + N_DIR * ssd_heads
    o_beta = o_qkv + 3 * gdn_w
    o_alpha = o_beta + N_DIR * gdn_heads

    w16 = w_in.astype(BF16)

    def small_cols(dd):
        parts = [w16[:, :, o_dt + dd * ssd_heads:o_dt + (dd + 1) * ssd_heads],
                 w16[:, :, o_beta + dd * gdn_heads:o_beta + (dd + 1) * gdn_heads],
                 w16[:, :, o_alpha + dd * gdn_heads:o_alpha + (dd + 1) * gdn_heads]]
        used = ssd_heads + 2 * gdn_heads
        return parts + [jnp.zeros((depth, d, LANE - used), BF16)]

    w_cat = jnp.concatenate(
        [w16[:, :, :d_mix], w16[:, :, o_xbc:o_xbc + ssd_w], w16[:, :, o_qkv:o_qkv + 3 * gdn_w],
         w16[:, :, o_xbc + ssd_w:o_xbc + ssd_w + bc_w]] + small_cols(0) + small_cols(1)
        + [jnp.zeros((depth, d, wp - c_sm - N_DIR * LANE), BF16)], axis=-1)

    def conv_layout(ssd_part, gdn_part):
        lead = ssd_part.shape[:-1]
        return jnp.concatenate(
            [jnp.zeros(lead + (d_mix,), F32), ssd_part[..., :ssd_w], gdn_part, ssd_part[..., ssd_w:],
             jnp.zeros(lead + (wp - c_sm,), F32)], axis=-1)

    cw_all = conv_layout(conv_ssd_w.astype(F32), conv_gdn_w.astype(F32))
    cw_all = jnp.concatenate([cw_all, jnp.zeros((depth, 8 - CONV_K, wp), F32)], axis=1)
    cb_all = conv_layout(conv_ssd_b.astype(F32)[:, None], jnp.zeros((depth, 1, 3 * gdn_w), F32))

    def lane_row(vals, lane0):
        return jnp.pad(vals, ((0, 0), (0, 0), (lane0, LANE - lane0 - vals.shape[-1])))

    zeros_rows = jnp.zeros((depth, N_DIR, 6, LANE), F32)
    ssd_par = jnp.concatenate([lane_row(ssd_dt_bias.astype(F32), 0)[:, :, None],
                               lane_row(-jnp.exp(ssd_a_log.astype(F32)), 0)[:, :, None], zeros_rows], axis=2)
    gdn_par = jnp.concatenate([lane_row(gdn_dt_bias.astype(F32), g_lane)[:, :, None],
                               lane_row(-jnp.exp(gdn_a_log.astype(F32)), g_lane)[:, :, None], zeros_rows], axis=2)
    dsk_all = jnp.repeat(ssd_d.astype(F32), SSD_HEAD_DIM, axis=-1)[:, None]
    gnw_all = jnp.tile(gdn_norm_w.astype(F32), (1, gdn_heads))[:, None]
    w_out16 = w_out.astype(BF16)
    head_of_lane = jnp.arange(LANE)[:, None]
    e_q = (jnp.arange(ssd_heads * CHUNK)[None] // CHUNK == head_of_lane).astype(BF16)

    rows = 16
    c_all = jnp.concatenate([c, c_ctx[None], jnp.zeros((rows - bsz - 1, d), c.dtype)], axis=0)
    mod = _modulation(c_all, w_ada, b_ada).reshape(depth, rows, 1, 3 * d)

    h_lat = x.reshape(bsz * seq, d)
    h_ctx = ctx.reshape(bsz * ctx_len, d)
    tm_lat = min(1024, seq)
    tm_ctx = min(1024, bsz * ctx_len)
    tm_out = 256
    conv_cols = (c_xs, c_sm)
    ssd_cols = (c_xs, c_bc, c_sm, ssd_w, bc_w)
    gdn_cols = (c_qkv, c_sm, gdn_w)
    ssd_state0 = jnp.zeros((bsz, N_DIR, SSD_GROUPS, SSD_STATE, ssd_w // SSD_GROUPS), F32)
    gdn_state0 = jnp.zeros((bsz, N_DIR, gdn_heads, GDN_HEAD_DIM, GDN_HEAD_DIM), F32)

    for l in range(depth):
        mod_lat = mod[l, :bsz, None]
        mod_ctx = mod[l, bsz:bsz + 1, None]
        pre_w = pre_norm_w[l][None]
        p_lat = _inproj(h_lat, mod_lat, mod_lat, pre_w, w_cat, l, cw_all[l], cb_all[l],
                        rows_per_mod=seq, group=GRID_W, tm=tm_lat, tn=tn, conv_cols=conv_cols)
        p_ctx = _inproj(h_ctx, mod_ctx, mod_ctx, pre_w, w_cat, l, cw_all[l], cb_all[l],
                        rows_per_mod=bsz * ctx_len, group=ctx_len, tm=tm_ctx, tn=tn, conv_cols=conv_cols)
        p_lat3 = p_lat.reshape(bsz, seq, wp)
        p_ctx3 = p_ctx.reshape(bsz, ctx_len, wp)

        y_ctx, ssd_state = _ssd_scan(p_ctx3, ssd_par[l], dsk_all[l], e_q,ssd_state0, cols=ssd_cols)
        y_lat, _ = _ssd_scan(p_lat3, ssd_par[l], dsk_all[l], e_q,ssd_state, cols=ssd_cols)
        o_ctx, gdn_state = _gdn_scan(p_ctx3, gdn_par[l], gdn_state0, cols=gdn_cols,
                                     beta_lane=beta_lane, g_lane=g_lane)
        o_lat, _ = _gdn_scan(p_lat3, gdn_par[l], gdn_state, cols=gdn_cols,
                             beta_lane=beta_lane, g_lane=g_lane)

        snw = ssd_norm_w[l][None].astype(F32)
        post_w = post_norm_w[l][None]
        h_lat_new = _outproj(y_lat, o_lat, p_lat, h_lat, mod_lat, w_out16, l, snw, gnw_all[l], post_w,
                             rows_per_mod=seq, tm=min(tm_out, seq))
        if l < depth - 1:
            h_ctx = _outproj(y_ctx, o_ctx, p_ctx, h_ctx, mod_ctx, w_out16, l, snw, gnw_all[l], post_w,
                             rows_per_mod=bsz * ctx_len, tm=min(tm_out, bsz * ctx_len))
        h_lat = h_lat_new
    return h_lat.reshape(bsz, seq, d)
```

```python
import functools

import jax
import jax.numpy as jnp
from jax import lax
from jax.experimental import pallas as pl
from jax.experimental.pallas import tpu as pltpu

F32 = jnp.float32
BF16 = jnp.bfloat16
HIGHEST = lax.Precision.HIGHEST

EPS = 1e-6
CHUNK = 128
GRID_W = 64
CONV_K = 5
N_DIR = 2
SSD_HEAD_DIM = 64
SSD_STATE = 128
SSD_GROUPS = 2
GDN_HEAD_DIM = 128
LANE = 128
SUBLANE = 8
NORM_UNROLL = 8
SCAN_OUT_DTYPE = jnp.bfloat16
SSD_CHUNKS_PER_STEP = 4
GDN_CHUNKS_PER_STEP = 4
GDN_SOLVE_BLOCK = 32
SUB_ROWS = 256
NORM_ROWS = 16
VMEM_LIMIT = 52 * 1024 * 1024

NT = (((1,), (1,)), ((), ()))
TN = (((0,), (0,)), ((), ()))


def _dot(a, b):
    return jnp.dot(a, b, preferred_element_type=F32)


def _dot_nt(a, b):
    return lax.dot_general(a, b, NT, preferred_element_type=F32)


def _dot_tn(a, b):
    return lax.dot_general(a, b, TN, preferred_element_type=F32)


def _sigmoid(x):
    return 1.0 / (1.0 + jnp.exp(-x))


def _silu(x):
    half = 0.5 * x
    return half + half * jnp.tanh(half)


def _softplus(x):
    return jnp.maximum(x, 0.0) + jnp.log(1.0 + jnp.exp(-jnp.abs(x)))


def _params(*sem, flags=None):
    return pltpu.CompilerParams(dimension_semantics=sem, vmem_limit_bytes=VMEM_LIMIT, flags=flags)


def _mod_kernel(c_ref, w_ref, b_ref, o_ref):
    a = _silu(c_ref[...])
    o_ref[0] = jnp.dot(a, w_ref[0], precision=HIGHEST, preferred_element_type=F32) + b_ref[0]


def _modulation(c_all, w_ada, b_ada):
    depth, d, n = w_ada.shape
    rows = c_all.shape[0]
    tn = 768
    return pl.pallas_call(
        _mod_kernel,
        name="adaln_mod",
        grid=(depth, n // tn),
        in_specs=[pl.BlockSpec((rows, d), lambda l, j: (0, 0)),
                  pl.BlockSpec((1, d, tn), lambda l, j: (l, 0, j)),
                  pl.BlockSpec((1, 1, tn), lambda l, j: (l, 0, j))],
        out_specs=pl.BlockSpec((1, rows, tn), lambda l, j: (l, 0, j)),
        out_shape=jax.ShapeDtypeStruct((depth, rows, n), F32),
        compiler_params=_params("parallel", "parallel"),
    )(c_all, w_ada, b_ada.reshape(depth, 1, n))


def _pack_kernel(a_ref, b_ref, o_ref, *, d_mix, ssd_w, bc_w, qkv_w, ssd_heads, gdn_heads):
    a = a_ref[0]
    b = b_ref[0]
    rows = a.shape[0]
    n_dt, n_g = N_DIR * ssd_heads, N_DIR * gdn_heads
    c_qkv = d_mix + ssd_w
    c_bc = c_qkv + qkv_w
    c_sm = c_bc + bc_w
    o_ref[0, :, 0:c_qkv] = a[:, 0:c_qkv].astype(BF16)
    o_ref[0, :, c_qkv:c_bc] = b[:, n_dt:n_dt + qkv_w].astype(BF16)
    o_ref[0, :, c_bc:c_sm] = a[:, c_qkv:c_qkv + bc_w].astype(BF16)
    for dd in range(N_DIR):
        beta0 = n_dt + qkv_w + dd * gdn_heads
        alpha0 = beta0 + n_g
        small = jnp.concatenate(
            [b[:, dd * ssd_heads:(dd + 1) * ssd_heads], b[:, beta0:beta0 + gdn_heads],
             b[:, alpha0:alpha0 + gdn_heads], jnp.zeros((rows, LANE - ssd_heads - 2 * gdn_heads), F32)], axis=1)
        o_ref[0, :, c_sm + dd * LANE:c_sm + (dd + 1) * LANE] = small.astype(BF16)
    pad0 = c_sm + N_DIR * LANE
    o_ref[0, :, pad0:] = jnp.zeros((rows, o_ref.shape[-1] - pad0), BF16)


def _pack_weights(w_in, wp, **dims):
    depth, d, in_dim = w_in.shape
    half = dims["d_mix"] + dims["ssd_w"] + dims["bc_w"]
    assert half % LANE == 0 and in_dim <= 2 * half
    rows = 256
    return pl.pallas_call(
        functools.partial(_pack_kernel, **dims),
        name="pack_w_in",
        grid=(depth, d // rows),
        in_specs=[pl.BlockSpec((1, rows, half), lambda l, r: (l, r, 0)),
                  pl.BlockSpec((1, rows, half), lambda l, r: (l, r, 1))],
        out_specs=pl.BlockSpec((1, rows, wp), lambda l, r: (l, r, 0)),
        out_shape=jax.ShapeDtypeStruct((depth, d, wp), BF16),
        compiler_params=_params("parallel", "parallel"),
    )(w_in, w_in)


def _inproj_kernel(h_ref, shift_ref, scale_ref, prew_ref, w_ref, cw_ref, cb_ref, o_ref, u_ref,
                   *, group, j_lo, j_hi, mixed_lanes):
    j = pl.program_id(1)
    tm, tn = o_ref.shape
    sub = max(group, min(tm, SUB_ROWS))

    @pl.when(j == 0)
    def _():
        prew = prew_ref[...]
        scale1p = 1.0 + scale_ref[0, 0]
        shift = shift_ref[0, 0]

        def norm_strip(s, carry):
            r0 = pl.multiple_of(s * NORM_ROWS, NORM_ROWS)
            x = h_ref[pl.ds(r0, NORM_ROWS), :]
            y = x * lax.rsqrt(jnp.mean(x * x, axis=-1, keepdims=True) + EPS) * prew
            u_ref[pl.ds(r0, NORM_ROWS), :] = (y * scale1p + shift).astype(BF16)
            return carry

        lax.fori_loop(0, tm // NORM_ROWS, norm_strip, 0, unroll=NORM_UNROLL)

    def conv_tile(lanes):
        cw = cw_ref[:, :lanes]
        bias = cb_ref[:, :lanes]
        row = lax.broadcasted_iota(jnp.int32, (SUBLANE, lanes), 0)
        offs = [k - CONV_K // 2 for k in range(CONV_K)]
        w_mid = [cw[k:k + 1] for k in range(CONV_K)]
        w_first = [jnp.where(row >= -off, w, 0.0) if off < 0 else w for off, w in zip(offs, w_mid)]
        w_last = [jnp.where(row < SUBLANE - off, w, 0.0) if off > 0 else w for off, w in zip(offs, w_mid)]
        last0 = group - SUBLANE

        def taps(pieces, weights, lo, hi):
            out = bias
            for x, w in zip(pieces, weights):
                out = out + x[lo:hi] * w
            return _silu(out)

        n_sub = tm // sub
        acc_next = _dot(u_ref[0:sub, :], w_ref[...])
        for sb in range(n_sub):
            acc = acc_next
            if sb + 1 < n_sub:
                acc_next = _dot(u_ref[(sb + 1) * sub:(sb + 2) * sub, :], w_ref[...])
            if lanes < tn:
                o_ref[sb * sub:(sb + 1) * sub, lanes:] = acc[:, lanes:]
            a = acc[:, :lanes]
            shifted = [a if off == 0 else pltpu.roll(a, (-off) % sub, axis=0) for off in offs]
            for gi in range(sub // group):
                g0 = gi * group
                o0 = sb * sub + g0
                o_ref[o0:o0 + SUBLANE, :lanes] = taps(shifted, w_first, g0, g0 + SUBLANE)
                o_ref[o0 + SUBLANE:o0 + last0, :lanes] = taps(shifted, w_mid, g0 + SUBLANE, g0 + last0)
                o_ref[o0 + last0:o0 + group, :lanes] = taps(shifted, w_last, g0 + last0, g0 + group)

    @pl.when((j < j_lo) if mixed_lanes else ((j < j_lo) | (j >= j_hi)))
    def _():
        o_ref[...] = _dot(u_ref[...], w_ref[...])

    @pl.when((j >= j_lo) & (j < j_hi))
    def _():
        conv_tile(tn)

    if mixed_lanes:
        @pl.when(j >= j_hi)
        def _():
            conv_tile(mixed_lanes)


def _inproj(h2d, shift, scale, pre_w, w_all, layer, cw, cb, *, rows_per_mod, group, tm, tn, conv_cols):
    m, d = h2d.shape
    wp = w_all.shape[-1]
    tiles_per_mod = rows_per_mod // tm
    kern = functools.partial(_inproj_kernel, group=group, j_lo=conv_cols[0] // tn, j_hi=conv_cols[1] // tn,
                             mixed_lanes=conv_cols[1] % tn)
    return pl.pallas_call(
        kern,
        name="inproj",
        grid=(m // tm, wp // tn),
        in_specs=[pl.BlockSpec((tm, d), lambda i, j: (i, 0)),
                  pl.BlockSpec((1, 1, 1, d), lambda i, j: (i // tiles_per_mod, 0, 0, 0)),
                  pl.BlockSpec((1, 1, 1, d), lambda i, j: (i // tiles_per_mod, 0, 0, 1)),
                  pl.BlockSpec((1, d), lambda i, j: (0, 0)),
                  pl.BlockSpec((None, d, tn), lambda i, j: (layer, 0, j)),
                  pl.BlockSpec((8, tn), lambda i, j: (0, j)),
                  pl.BlockSpec((1, tn), lambda i, j: (0, j))],
        out_specs=pl.BlockSpec((tm, tn), lambda i, j: (i, j)),
        out_shape=jax.ShapeDtypeStruct((m, wp), F32),
        scratch_shapes=[pltpu.VMEM((tm, d), BF16)],
        compiler_params=_params("parallel", "arbitrary"),
    )(h2d, shift, scale, pre_w, w_all, cw, cb)


def _direction_masks(d):
    ri = lax.broadcasted_iota(jnp.int32, (CHUNK, CHUNK), 0)
    ci = lax.broadcasted_iota(jnp.int32, (CHUNK, CHUNK), 1)
    diff = (ri - ci) * (1 - 2 * d)
    return diff >= 0, diff > 0, ri, ci


def _chunk_index(d, p, nc):
    return p + d * (nc - 1 - 2 * p)


def _ssd_kernel(xs_ref, bc_ref, sm_ref, par_ref, dsk_ref, eq_ref, h0_ref, y_ref, hT_ref):
    d = pl.program_id(1)
    p = pl.program_id(2)
    heads_per_group = xs_ref.shape[-1] // (SSD_GROUPS * SSD_HEAD_DIM)
    gw = heads_per_group * SSD_HEAD_DIM

    @pl.when(p == 0)
    def _():
        hT_ref[...] = h0_ref[...]

    incl, _, _, _ = _direction_masks(d)
    fwd = d == 0
    tri = incl.astype(F32)
    par = par_ref[0]
    skip = dsk_ref[...] * (1 - d).astype(F32)
    cps = xs_ref.shape[1] // CHUNK
    rows = [pl.multiple_of((s + d * (cps - 1 - 2 * s)) * CHUNK, CHUNK) for s in range(cps)]
    groups = range(SSD_GROUPS)
    ppg = gw // LANE
    pairs = range(SSD_GROUPS * ppg)
    left = lax.broadcasted_iota(jnp.int32, (CHUNK, LANE), 1) < SSD_HEAD_DIM

    cg, xp, yd, eacum_x, cdec, upd = [], [], [], [], [], []
    for s in range(cps):
        sm = sm_ref[0, pl.ds(rows[s], CHUNK), :]
        xs = xs_ref[0, pl.ds(rows[s], CHUNK), :]
        bc = bc_ref[0, pl.ds(rows[s], CHUNK), :]
        dt = _softplus(sm + par[0:1])
        loga = dt * par[1:2]
        acum = jnp.dot(tri, loga, precision=HIGHEST, preferred_element_type=F32)
        dt_t = dt.T
        hi = acum.astype(BF16)
        lo = (acum - hi.astype(F32)).astype(BF16)
        acum_q = _dot(hi, eq_ref[...]) + _dot(lo, eq_ref[...])
        acum_t = (hi.astype(F32) + lo.astype(F32)).T
        bg = [bc[:, g * SSD_STATE:(g + 1) * SSD_STATE].astype(BF16) for g in groups]
        cgs = [bc[:, (SSD_GROUPS + g) * SSD_STATE:(SSD_GROUPS + g + 1) * SSD_STATE].astype(BF16) for g in groups]
        cb = [_dot_nt(cgs[g], bg[g]) for g in groups]

        def intra(hh):
            seg = acum_q[:, hh * CHUNK:(hh + 1) * CHUNK] - acum_t[hh:hh + 1, :]
            return (cb[hh // heads_per_group] * jnp.where(incl, jnp.exp(seg), 0.0)
                    * dt_t[hh:hh + 1, :]).astype(BF16)

        xps = [xs[:, i * LANE:(i + 1) * LANE] for i in pairs]
        m_pair = [jnp.concatenate([intra(2 * i), intra(2 * i + 1)], axis=1) for i in pairs]
        x_bd = [jnp.concatenate([jnp.where(left, x, 0.0), jnp.where(left, 0.0, x)], axis=0).astype(BF16)
                for x in xps]
        yd.append([_dot(m_pair[i], x_bd[i]) for i in pairs])
        acum_x = [jnp.where(left, acum_q[:, 2 * i * CHUNK:2 * i * CHUNK + LANE],
                            acum_q[:, (2 * i + 1) * CHUNK:(2 * i + 1) * CHUNK + LANE]) for i in pairs]
        alast_x = [jnp.where(fwd, a[CHUNK - 1:CHUNK], a[0:1]) for a in acum_x]
        dt_x = [jnp.where(left, dt[:, 2 * i:2 * i + 1], dt[:, 2 * i + 1:2 * i + 2]) for i in pairs]
        xw = [(xps[i] * (dt_x[i] * jnp.exp(alast_x[i] - acum_x[i]))).astype(BF16) for i in pairs]
        upd.append([_dot_tn(bg[g], jnp.concatenate(xw[g * ppg:(g + 1) * ppg], axis=1)) for g in groups])
        cdec.append([jnp.exp(jnp.concatenate(alast_x[g * ppg:(g + 1) * ppg], axis=1)) for g in groups])
        eacum_x.append([jnp.exp(a) for a in acum_x])
        cg.append(cgs)
        xp.append(xps)

    h_t = [hT_ref[0, 0, g] for g in groups]
    for s in range(cps):
        yoff = [_dot(cg[s][g], h_t[g].astype(BF16)) for g in groups]
        h_t = [h_t[g] * cdec[s][g] + upd[s][g] for g in groups]
        for i in pairs:
            lo = i * LANE
            yo = yoff[i // ppg][:, (i % ppg) * LANE:(i % ppg + 1) * LANE]
            y_ref[0, 0, pl.ds(rows[s], CHUNK), lo:lo + LANE] = (
                yd[s][i] + yo * eacum_x[s][i] + xp[s][i] * skip[:, lo:lo + LANE]).astype(y_ref.dtype)
    for g in groups:
        hT_ref[0, 0, g] = h_t[g]


def _ssd_scan(p3d, par, dsk, e_q, h0, *, cols):
    bsz, length, _ = p3d.shape
    nc = length // CHUNK
    c_xs, c_bc, c_sm, w_xs, w_bc = cols
    cps = min(SSD_CHUNKS_PER_STEP, nc)
    nb = nc // cps
    rows = cps * CHUNK
    chunk = functools.partial(_chunk_index, nc=nb)
    state_shape = h0.shape
    return pl.pallas_call(
        _ssd_kernel,
        name="ssd_scan",
        grid=(bsz, N_DIR, nb),
        in_specs=[pl.BlockSpec((1, rows, w_xs), lambda b, d, p: (b, chunk(d, p), c_xs // w_xs)),
                  pl.BlockSpec((1, rows, w_bc), lambda b, d, p: (b, chunk(d, p), c_bc // w_bc)),
                  pl.BlockSpec((1, rows, LANE), lambda b, d, p: (b, chunk(d, p), c_sm // LANE + d)),
                  pl.BlockSpec((1, 8, LANE), lambda b, d, p: (d, 0, 0)),
                  pl.BlockSpec((1, w_xs), lambda b, d, p: (0, 0)),
                  pl.BlockSpec(e_q.shape, lambda b, d, p: (0, 0)),
                  pl.BlockSpec((1, 1) + state_shape[2:], lambda b, d, p: (b, d, 0, 0, 0))],
        out_specs=[pl.BlockSpec((1, 1, rows, w_xs), lambda b, d, p: (d, b, chunk(d, p), 0)),
                   pl.BlockSpec((1, 1) + state_shape[2:], lambda b, d, p: (b, d, 0, 0, 0))],
        out_shape=[jax.ShapeDtypeStruct((N_DIR, bsz, length, w_xs), SCAN_OUT_DTYPE),
                   jax.ShapeDtypeStruct(state_shape, F32)],
        compiler_params=_params("parallel", "parallel", "arbitrary"),
    )(p3d, p3d, p3d, par, dsk, e_q, h0)


def _diag_block_inverses(a_list, xor_ij, block):
    eye = jnp.where(xor_ij == 0, 1.0, 0.0)
    in_pair = xor_ij < 2
    t_list = [eye - jnp.where(in_pair, a, 0.0) for a in a_list]
    size = 2
    while size < block:
        couples = (xor_ij >> (size.bit_length() - 1)) == 1
        a_off = [jnp.where(couples, a, 0.0).astype(BF16) for a in a_list]
        t16 = [t.astype(BF16) for t in t_list]
        ta = [_dot(t, a).astype(BF16) for t, a in zip(t16, a_off)]
        t_list = [t - _dot(x, tb) for t, x, tb in zip(t_list, ta, t16)]
        size *= 2
    return t_list


def _block_substitution(a_list, t_list, rhs_list, block, backward):
    n = a_list[0].shape[0]
    width = rhs_list[0].shape[1]
    order = range(n // block)
    order = list(reversed(order)) if backward else list(order)
    a16 = [a.astype(BF16) for a in a_list]
    t16 = [t.astype(BF16) for t in t_list]

    def place(y, i):
        parts = []
        if i > 0:
            parts.append(jnp.zeros((i * block, width), y.dtype))
        parts.append(y)
        if (i + 1) * block < n:
            parts.append(jnp.zeros((n - (i + 1) * block, width), y.dtype))
        return jnp.concatenate(parts, axis=0) if len(parts) > 1 else y

    x_list = [None] * len(a_list)
    for step, i in enumerate(order):
        rows = slice(i * block, (i + 1) * block)
        y = [r[rows] for r in rhs_list]
        if step > 0:
            y = [yy - _dot(a[rows], x.astype(BF16)) for yy, a, x in zip(y, a16, x_list)]
        xi = [_dot(t[rows], place(yy.astype(BF16), i)) for t, yy in zip(t16, y)]
        x_list = [place(v, i) if x is None else x + place(v, i) for x, v in zip(x_list, xi)]
    return x_list


def _gdn_kernel(q_ref, k_ref, v_ref, sm_ref, par_ref, s0_ref, o_ref, s_ref, *, beta_lane, g_lane):
    d = pl.program_id(1)
    p = pl.program_id(2)
    heads = range(q_ref.shape[-1] // GDN_HEAD_DIM)

    @pl.when(p == 0)
    def _():
        s_ref[...] = s0_ref[...]

    par = par_ref[0]
    kk = GDN_HEAD_DIM
    cps = q_ref.shape[1] // CHUNK
    units = [(s, h) for s in range(cps) for h in heads]
    ri = lax.broadcasted_iota(jnp.int32, (CHUNK, CHUNK), 0)
    ci = lax.broadcasted_iota(jnp.int32, (CHUNK, CHUNK), 1)
    xor_ij = ri ^ ci

    def col(x, lane):
        return x[:, lane:lane + 1]

    def scan(backward):
        incl = (ri <= ci) if backward else (ri >= ci)
        strict = (ri < ci) if backward else (ri > ci)
        tri = incl.astype(F32)
        last = 0 if backward else CHUNK - 1
        rows = [((cps - 1 - s) if backward else s) * CHUNK for s in range(cps)]

        def head_cols(ref, s, h):
            return ref[0, rows[s]:rows[s] + CHUNK, h * kk:(h + 1) * kk]

        eg, eend, cdec, beta_all, gcum, gcum_t = [], [], [], [], [], []
        for s in range(cps):
            sm = sm_ref[0, rows[s]:rows[s] + CHUNK, :]
            beta_all.append(_sigmoid(sm))
            g_all = par[1:2] * _softplus(sm + par[0:1])
            gc = jnp.dot(tri, g_all, precision=HIGHEST, preferred_element_type=F32)
            glast = gc[last:last + 1]
            gcum.append(gc)
            gcum_t.append(gc.T)
            eg.append(jnp.exp(gc))
            eend.append(jnp.exp(glast - gc))
            cdec.append(jnp.exp(glast))
        qn, kn = {}, {}
        for s, h in units:
            q = head_cols(q_ref, s, h)
            k = head_cols(k_ref, s, h)
            qn[s, h] = q * (lax.rsqrt(jnp.sum(q * q, axis=-1, keepdims=True) + EPS) * kk ** -0.5)
            kn[s, h] = k * lax.rsqrt(jnp.sum(k * k, axis=-1, keepdims=True) + EPS)
        beta = {(s, h): col(beta_all[s], beta_lane + h) for s, h in units}
        dec = {(s, h): jnp.where(
            incl, jnp.exp(col(gcum[s], g_lane + h) - gcum_t[s][g_lane + h:g_lane + h + 1, :]), 0.0)
            for s, h in units}
        kb = {x: kn[x] * beta[x] for x in units}
        k16 = {x: kn[x].astype(BF16) for x in units}
        a = [jnp.where(strict, _dot_nt(kb[x].astype(BF16), k16[x]) * dec[x], 0.0) for x in units]
        attn = {x: (_dot_nt(qn[x].astype(BF16), k16[x]) * dec[x]).astype(BF16) for x in units}
        rhs = [jnp.concatenate([head_cols(v_ref, s, h) * beta[s, h], kb[s, h] * col(eg[s], g_lane + h)], axis=1)
               for s, h in units]
        t = _diag_block_inverses(a, xor_ij, GDN_SOLVE_BLOCK)
        sol = dict(zip(units, _block_substitution(a, t, rhs, GDN_SOLVE_BLOCK, backward)))
        wq = {(s, h): jnp.concatenate([sol[s, h][:, kk:].astype(BF16),
                                       (qn[s, h] * col(eg[s], g_lane + h)).astype(BF16)], axis=0)
              for s, h in units}
        ke = {(s, h): (kn[s, h] * col(eend[s], g_lane + h)).astype(BF16) for s, h in units}

        state = [s_ref[0, 0, h] for h in heads]
        for s in range(cps):
            ws = [_dot(wq[s, h], state[h].astype(BF16)) for h in heads]
            vnew = [(sol[s, h][:, :kk] - ws[h][:CHUNK]).astype(BF16) for h in heads]
            o = [ws[h][CHUNK:] + _dot(attn[s, h], vnew[h]) for h in heads]
            state = [state[h] * col(cdec[s], g_lane + h) + _dot_tn(ke[s, h], vnew[h]) for h in heads]
            for h in heads:
                o_ref[0, 0, rows[s]:rows[s] + CHUNK, h * kk:(h + 1) * kk] = o[h].astype(o_ref.dtype)
        for h in heads:
            s_ref[0, 0, h] = state[h]

    @pl.when(d == 0)
    def _():
        scan(False)

    @pl.when(d == 1)
    def _():
        scan(True)


def _gdn_scan(p3d, par, s0, *, cols, beta_lane, g_lane):
    bsz, length, _ = p3d.shape
    nc = length // CHUNK
    c_q, c_sm, w_q = cols
    cps = min(GDN_CHUNKS_PER_STEP, nc)
    nb = nc // cps
    rows = cps * CHUNK
    chunk = functools.partial(_chunk_index, nc=nb)
    state_shape = s0.shape
    kern = functools.partial(_gdn_kernel, beta_lane=beta_lane, g_lane=g_lane)
    qkv_spec = [pl.BlockSpec((1, rows, w_q), functools.partial(
        lambda b, d, p, i: (b, chunk(d, p), c_q // w_q + i), i=i)) for i in range(3)]
    return pl.pallas_call(
        kern,
        name="gdn_scan",
        grid=(bsz, N_DIR, nb),
        in_specs=qkv_spec + [
            pl.BlockSpec((1, rows, LANE), lambda b, d, p: (b, chunk(d, p), c_sm // LANE + d)),
            pl.BlockSpec((1, 8, LANE), lambda b, d, p: (d, 0, 0)),
            pl.BlockSpec((1, 1) + state_shape[2:], lambda b, d, p: (b, d, 0, 0, 0))],
        out_specs=[pl.BlockSpec((1, 1, rows, w_q), lambda b, d, p: (d, b, chunk(d, p), 0)),
                   pl.BlockSpec((1, 1) + state_shape[2:], lambda b, d, p: (b, d, 0, 0, 0))],
        out_shape=[jax.ShapeDtypeStruct((N_DIR, bsz, length, w_q), SCAN_OUT_DTYPE),
                   jax.ShapeDtypeStruct(state_shape, F32)],
        compiler_params=_params("parallel", "parallel", "arbitrary"),
    )(p3d, p3d, p3d, p3d, par, s0)


def _outproj_kernel(y0_ref, y1_ref, o0_ref, o1_ref, z_ref, h_ref, gate_ref, w_ref,
                    snw_ref, gnw_ref, postw_ref, out_ref):
    zs = _silu(z_ref[...])
    ssd_w = y0_ref.shape[-1]
    gw = ssd_w // SSD_GROUPS
    ys = (y0_ref[0].astype(F32) + y1_ref[0].astype(F32)) * zs[:, :ssd_w]
    parts = []
    for g in range(SSD_GROUPS):
        yg = ys[:, g * gw:(g + 1) * gw]
        parts.append(yg * lax.rsqrt(jnp.mean(yg * yg, axis=-1, keepdims=True) + EPS))
    ysn = jnp.concatenate(parts, axis=1) * snw_ref[...]
    og = o0_ref[0].astype(F32) + o1_ref[0].astype(F32)
    parts = []
    for h in range(og.shape[-1] // GDN_HEAD_DIM):
        oh = og[:, h * GDN_HEAD_DIM:(h + 1) * GDN_HEAD_DIM]
        parts.append(oh * lax.rsqrt(jnp.mean(oh * oh, axis=-1, keepdims=True) + EPS))
    ogn = jnp.concatenate(parts, axis=1) * gnw_ref[...] * zs[:, ssd_w:]
    ycat = jnp.concatenate([ysn, ogn], axis=1).astype(BF16)
    m = _dot(ycat, w_ref[...])
    mn = m * lax.rsqrt(jnp.mean(m * m, axis=-1, keepdims=True) + EPS) * postw_ref[...]
    out_ref[...] = h_ref[...] + gate_ref[0, 0] * mn


def _outproj(y, o, p2d, h2d, gate, w_all, layer, snw, gnw, postw, *, rows_per_mod, tm):
    m, d = h2d.shape
    ssd_w = y.shape[-1]
    gdn_w = o.shape[-1]
    tiles_per_mod = rows_per_mod // tm
    y2 = y.reshape(N_DIR, m, ssd_w)
    o2 = o.reshape(N_DIR, m, gdn_w)
    return pl.pallas_call(
        _outproj_kernel,
        name="outproj",
        grid=(m // tm,),
        in_specs=[pl.BlockSpec((1, tm, ssd_w), lambda i: (0, i, 0)),
                  pl.BlockSpec((1, tm, ssd_w), lambda i: (1, i, 0)),
                  pl.BlockSpec((1, tm, gdn_w), lambda i: (0, i, 0)),
                  pl.BlockSpec((1, tm, gdn_w), lambda i: (1, i, 0)),
                  pl.BlockSpec((tm, ssd_w + gdn_w), lambda i: (i, 0)),
                  pl.BlockSpec((tm, d), lambda i: (i, 0)),
                  pl.BlockSpec((1, 1, 1, d), lambda i: (i // tiles_per_mod, 0, 0, 2)),
                  pl.BlockSpec((None, ssd_w + gdn_w, d), lambda i: (layer, 0, 0)),
                  pl.BlockSpec((1, ssd_w), lambda i: (0, 0)),
                  pl.BlockSpec((1, gdn_w), lambda i: (0, 0)),
                  pl.BlockSpec((1, d), lambda i: (0, 0))],
        out_specs=pl.BlockSpec((tm, d), lambda i: (i, 0)),
        out_shape=jax.ShapeDtypeStruct((m, d), F32),
        compiler_params=_params("parallel"),
    )(y2, y2, o2, o2, p2d, h2d, gate, w_all, snw, gnw, postw)


def kernel(x, c, ctx, c_ctx, w_ada, b_ada, pre_norm_w, post_norm_w, w_in, conv_ssd_w, conv_ssd_b,
           conv_gdn_w, ssd_a_log, ssd_dt_bias, ssd_d, ssd_norm_w, gdn_a_log, gdn_dt_bias,
           gdn_norm_w, w_out):
    bsz, seq, d = x.shape
    ctx_len = ctx.shape[1]
    depth = w_in.shape[0]
    d_mix = w_out.shape[1]
    ssd_w = d_mix // 2
    gdn_w = d_mix - ssd_w
    ssd_heads = ssd_w // SSD_HEAD_DIM
    gdn_heads = gdn_w // GDN_HEAD_DIM
    bc_w = 2 * SSD_GROUPS * SSD_STATE
    assert ssd_w == gdn_w and ssd_heads + 2 * gdn_heads <= LANE
    assert seq % CHUNK == 0 and ctx_len % CHUNK == 0 and CHUNK % GRID_W == 0

    c_z, c_xs = 0, d_mix
    c_qkv = c_xs + ssd_w
    c_bc = c_qkv + 3 * gdn_w
    c_sm = c_bc + bc_w
    tn = 1024
    wp = -(-(c_sm + N_DIR * LANE) // tn) * tn
    assert c_qkv % gdn_w == 0 and c_xs % ssd_w == 0 and c_bc % bc_w == 0 and c_sm % LANE == 0
    assert c_xs % tn == 0 and c_sm // tn == wp // tn - 1
    beta_lane, g_lane = ssd_heads, ssd_heads + gdn_heads

    o_xbc = d_mix
    o_dt = o_xbc + ssd_w + bc_w
    o_qkv = o_dt + N_DIR * ssd_heads
    o_beta = o_qkv + 3 * gdn_w
    o_alpha = o_beta + N_DIR * gdn_heads

    assert w_in.shape[-1] == o_alpha + N_DIR * gdn_heads
    w_cat = _pack_weights(w_in, wp, d_mix=d_mix, ssd_w=ssd_w, bc_w=bc_w, qkv_w=3 * gdn_w,
                          ssd_heads=ssd_heads, gdn_heads=gdn_heads)

    def conv_layout(ssd_part, gdn_part):
        lead = ssd_part.shape[:-1]
        return jnp.concatenate(
            [jnp.zeros(lead + (d_mix,), F32), ssd_part[..., :ssd_w], gdn_part, ssd_part[..., ssd_w:],
             jnp.zeros(lead + (wp - c_sm,), F32)], axis=-1)

    cw_all = conv_layout(conv_ssd_w.astype(F32), conv_gdn_w.astype(F32))
    cw_all = jnp.concatenate([cw_all, jnp.zeros((depth, 8 - CONV_K, wp), F32)], axis=1)
    cb_all = conv_layout(conv_ssd_b.astype(F32)[:, None], jnp.zeros((depth, 1, 3 * gdn_w), F32))

    def lane_row(vals, lane0):
        return jnp.pad(vals, ((0, 0), (0, 0), (lane0, LANE - lane0 - vals.shape[-1])))

    zeros_rows = jnp.zeros((depth, N_DIR, 6, LANE), F32)
    ssd_par = jnp.concatenate([lane_row(ssd_dt_bias.astype(F32), 0)[:, :, None],
                               lane_row(-jnp.exp(ssd_a_log.astype(F32)), 0)[:, :, None], zeros_rows], axis=2)
    gdn_par = jnp.concatenate([lane_row(gdn_dt_bias.astype(F32), g_lane)[:, :, None],
                               lane_row(-jnp.exp(gdn_a_log.astype(F32)), g_lane)[:, :, None], zeros_rows], axis=2)
    dsk_all = jnp.repeat(ssd_d.astype(F32), SSD_HEAD_DIM, axis=-1)[:, None]
    gnw_all = jnp.tile(gdn_norm_w.astype(F32), (1, gdn_heads))[:, None]
    w_out16 = w_out.astype(BF16)
    head_of_lane = jnp.arange(LANE)[:, None]
    e_q = (jnp.arange(ssd_heads * CHUNK)[None] // CHUNK == head_of_lane).astype(BF16)

    rows = 16
    c_all = jnp.concatenate([c, c_ctx[None], jnp.zeros((rows - bsz - 1, d), c.dtype)], axis=0)
    mod = _modulation(c_all, w_ada, b_ada).reshape(depth, rows, 1, 3 * d)

    h_lat = x.reshape(bsz * seq, d)
    h_ctx = ctx.reshape(bsz * ctx_len, d)
    tm_lat = min(1024, seq)
    tm_ctx = min(1024, bsz * ctx_len)
    tm_out = 256
    conv_cols = (c_xs, c_sm)
    ssd_cols = (c_xs, c_bc, c_sm, ssd_w, bc_w)
    gdn_cols = (c_qkv, c_sm, gdn_w)
    ssd_state0 = jnp.zeros((bsz, N_DIR, SSD_GROUPS, SSD_STATE, ssd_w // SSD_GROUPS), F32)
    gdn_state0 = jnp.zeros((bsz, N_DIR, gdn_heads, GDN_HEAD_DIM, GDN_HEAD_DIM), F32)

    for l in range(depth):
        mod_lat = mod[l, :bsz, None]
        mod_ctx = mod[l, bsz:bsz + 1, None]
        pre_w = pre_norm_w[l][None]
        p_lat = _inproj(h_lat, mod_lat, mod_lat, pre_w, w_cat, l, cw_all[l], cb_all[l],
                        rows_per_mod=seq, group=GRID_W, tm=tm_lat, tn=tn, conv_cols=conv_cols)
        p_ctx = _inproj(h_ctx, mod_ctx, mod_ctx, pre_w, w_cat, l, cw_all[l], cb_all[l],
                        rows_per_mod=bsz * ctx_len, group=ctx_len, tm=tm_ctx, tn=tn, conv_cols=conv_cols)
        p_lat3 = p_lat.reshape(bsz, seq, wp)
        p_ctx3 = p_ctx.reshape(bsz, ctx_len, wp)

        y_ctx, ssd_state = _ssd_scan(p_ctx3, ssd_par[l], dsk_all[l], e_q,ssd_state0, cols=ssd_cols)
        y_lat, _ = _ssd_scan(p_lat3, ssd_par[l], dsk_all[l], e_q,ssd_state, cols=ssd_cols)
        o_ctx, gdn_state = _gdn_scan(p_ctx3, gdn_par[l], gdn_state0, cols=gdn_cols,
                                     beta_lane=beta_lane, g_lane=g_lane)
        o_lat, _ = _gdn_scan(p_lat3, gdn_par[l], gdn_state, cols=gdn_cols,
                             beta_lane=beta_lane, g_lane=g_lane)

        snw = ssd_norm_w[l][None].astype(F32)
        post_w = post_norm_w[l][None]
        h_lat_new = _outproj(y_lat, o_lat, p_lat, h_lat, mod_lat, w_out16, l, snw, gnw_all[l], post_w,
                             rows_per_mod=seq, tm=min(tm_out, seq))
        if l < depth - 1:
            h_ctx = _outproj(y_ctx, o_ctx, p_ctx, h_ctx, mod_ctx, w_out16, l, snw, gnw_all[l], post_w,
                             rows_per_mod=bsz * ctx_len, tm=min(tm_out, bsz * ctx_len))
        h_lat = h_lat_new
    return h_lat.reshape(bsz, seq, d)
```

```python
import functools

import jax
import jax.numpy as jnp
from jax import lax
from jax.experimental import pallas as pl
from jax.experimental.pallas import tpu as pltpu

F32 = jnp.float32
BF16 = jnp.bfloat16
HIGHEST = lax.Precision.HIGHEST

EPS = 1e-6
CHUNK = 128
GRID_W = 64
CONV_K = 5
N_DIR = 2
SSD_HEAD_DIM = 64
SSD_STATE = 128
SSD_GROUPS = 2
GDN_HEAD_DIM = 128
LANE = 128
SUBLANE = 8
NORM_UNROLL = 8
SCAN_OUT_DTYPE = jnp.bfloat16
CHUNKS_PER_STEP = 4
GDN_SOLVE_BLOCK = 32
SUB_ROWS = 256
NORM_ROWS = 16
VMEM_LIMIT = 52 * 1024 * 1024

NT = (((1,), (1,)), ((), ()))
TN = (((0,), (0,)), ((), ()))


def _dot(a, b):
    return jnp.dot(a, b, preferred_element_type=F32)


def _dot_nt(a, b):
    return lax.dot_general(a, b, NT, preferred_element_type=F32)


def _dot_tn(a, b):
    return lax.dot_general(a, b, TN, preferred_element_type=F32)


def _sigmoid(x):
    return 1.0 / (1.0 + jnp.exp(-x))


def _silu(x):
    half = 0.5 * x
    return half + half * jnp.tanh(half)


def _softplus(x):
    return jnp.maximum(x, 0.0) + jnp.log(1.0 + jnp.exp(-jnp.abs(x)))


def _params(*sem, flags=None):
    return pltpu.CompilerParams(dimension_semantics=sem, vmem_limit_bytes=VMEM_LIMIT, flags=flags)


def _mod_kernel(c_ref, w_ref, b_ref, o_ref):
    a = _silu(c_ref[...])
    o_ref[0] = jnp.dot(a, w_ref[0], precision=HIGHEST, preferred_element_type=F32) + b_ref[0]


def _modulation(c_all, w_ada, b_ada):
    depth, d, n = w_ada.shape
    rows = c_all.shape[0]
    tn = 768
    return pl.pallas_call(
        _mod_kernel,
        name="adaln_mod",
        grid=(depth, n // tn),
        in_specs=[pl.BlockSpec((rows, d), lambda l, j: (0, 0)),
                  pl.BlockSpec((1, d, tn), lambda l, j: (l, 0, j)),
                  pl.BlockSpec((1, 1, tn), lambda l, j: (l, 0, j))],
        out_specs=pl.BlockSpec((1, rows, tn), lambda l, j: (l, 0, j)),
        out_shape=jax.ShapeDtypeStruct((depth, rows, n), F32),
        compiler_params=_params("parallel", "parallel"),
    )(c_all, w_ada, b_ada.reshape(depth, 1, n))


def _pack_kernel(a_ref, b_ref, o_ref, *, d_mix, ssd_w, bc_w, qkv_w, ssd_heads, gdn_heads):
    a = a_ref[0]
    b = b_ref[0]
    rows = a.shape[0]
    n_dt, n_g = N_DIR * ssd_heads, N_DIR * gdn_heads
    c_qkv = d_mix + ssd_w
    c_bc = c_qkv + qkv_w
    c_sm = c_bc + bc_w
    o_ref[0, :, 0:c_qkv] = a[:, 0:c_qkv].astype(BF16)
    o_ref[0, :, c_qkv:c_bc] = b[:, n_dt:n_dt + qkv_w].astype(BF16)
    o_ref[0, :, c_bc:c_sm] = a[:, c_qkv:c_qkv + bc_w].astype(BF16)
    for dd in range(N_DIR):
        beta0 = n_dt + qkv_w + dd * gdn_heads
        alpha0 = beta0 + n_g
        small = jnp.concatenate(
            [b[:, dd * ssd_heads:(dd + 1) * ssd_heads], b[:, beta0:beta0 + gdn_heads],
             b[:, alpha0:alpha0 + gdn_heads], jnp.zeros((rows, LANE - ssd_heads - 2 * gdn_heads), F32)], axis=1)
        o_ref[0, :, c_sm + dd * LANE:c_sm + (dd + 1) * LANE] = small.astype(BF16)
    pad0 = c_sm + N_DIR * LANE
    o_ref[0, :, pad0:] = jnp.zeros((rows, o_ref.shape[-1] - pad0), BF16)


def _pack_weights(w_in, wp, **dims):
    depth, d, in_dim = w_in.shape
    half = dims["d_mix"] + dims["ssd_w"] + dims["bc_w"]
    assert half % LANE == 0 and in_dim <= 2 * half
    rows = 256
    return pl.pallas_call(
        functools.partial(_pack_kernel, **dims),
        name="pack_w_in",
        grid=(depth, d // rows),
        in_specs=[pl.BlockSpec((1, rows, half), lambda l, r: (l, r, 0)),
                  pl.BlockSpec((1, rows, half), lambda l, r: (l, r, 1))],
        out_specs=pl.BlockSpec((1, rows, wp), lambda l, r: (l, r, 0)),
        out_shape=jax.ShapeDtypeStruct((depth, d, wp), BF16),
        compiler_params=_params("parallel", "parallel"),
    )(w_in, w_in)


def _inproj_kernel(h_ref, shift_ref, scale_ref, prew_ref, w_ref, cw_ref, cb_ref, o_ref, u_ref,
                   *, group, j_lo, j_hi, mixed_lanes):
    j = pl.program_id(1)
    tm, tn = o_ref.shape
    sub = max(group, min(tm, SUB_ROWS))

    @pl.when(j == 0)
    def _():
        prew = prew_ref[...]
        scale1p = 1.0 + scale_ref[0, 0]
        shift = shift_ref[0, 0]

        def norm_strip(s, carry):
            r0 = pl.multiple_of(s * NORM_ROWS, NORM_ROWS)
            x = h_ref[pl.ds(r0, NORM_ROWS), :]
            y = x * lax.rsqrt(jnp.mean(x * x, axis=-1, keepdims=True) + EPS) * prew
            u_ref[pl.ds(r0, NORM_ROWS), :] = (y * scale1p + shift).astype(BF16)
            return carry

        lax.fori_loop(0, tm // NORM_ROWS, norm_strip, 0, unroll=NORM_UNROLL)

    def conv_tile(lanes):
        cw = cw_ref[:, :lanes]
        bias = cb_ref[:, :lanes]
        row = lax.broadcasted_iota(jnp.int32, (SUBLANE, lanes), 0)
        offs = [k - CONV_K // 2 for k in range(CONV_K)]
        w_mid = [cw[k:k + 1] for k in range(CONV_K)]
        w_first = [jnp.where(row >= -off, w, 0.0) if off < 0 else w for off, w in zip(offs, w_mid)]
        w_last = [jnp.where(row < SUBLANE - off, w, 0.0) if off > 0 else w for off, w in zip(offs, w_mid)]
        last0 = group - SUBLANE

        def taps(pieces, weights, lo, hi):
            out = bias
            for x, w in zip(pieces, weights):
                out = out + x[lo:hi] * w
            return _silu(out)

        n_sub = tm // sub
        acc_next = _dot(u_ref[0:sub, :], w_ref[...])
        for sb in range(n_sub):
            acc = acc_next
            if sb + 1 < n_sub:
                acc_next = _dot(u_ref[(sb + 1) * sub:(sb + 2) * sub, :], w_ref[...])
            if lanes < tn:
                o_ref[sb * sub:(sb + 1) * sub, lanes:] = acc[:, lanes:]
            a = acc[:, :lanes]
            shifted = [a if off == 0 else pltpu.roll(a, (-off) % sub, axis=0) for off in offs]
            for gi in range(sub // group):
                g0 = gi * group
                o0 = sb * sub + g0
                o_ref[o0:o0 + SUBLANE, :lanes] = taps(shifted, w_first, g0, g0 + SUBLANE)
                o_ref[o0 + SUBLANE:o0 + last0, :lanes] = taps(shifted, w_mid, g0 + SUBLANE, g0 + last0)
                o_ref[o0 + last0:o0 + group, :lanes] = taps(shifted, w_last, g0 + last0, g0 + group)

    @pl.when((j < j_lo) if mixed_lanes else ((j < j_lo) | (j >= j_hi)))
    def _():
        o_ref[...] = _dot(u_ref[...], w_ref[...])

    @pl.when((j >= j_lo) & (j < j_hi))
    def _():
        conv_tile(tn)

    if mixed_lanes:
        @pl.when(j >= j_hi)
        def _():
            conv_tile(mixed_lanes)


def _inproj(h2d, shift, scale, pre_w, w_all, layer, cw, cb, *, rows_per_mod, group, tm, tn, conv_cols):
    m, d = h2d.shape
    wp = w_all.shape[-1]
    tiles_per_mod = rows_per_mod // tm
    kern = functools.partial(_inproj_kernel, group=group, j_lo=conv_cols[0] // tn, j_hi=conv_cols[1] // tn,
                             mixed_lanes=conv_cols[1] % tn)
    return pl.pallas_call(
        kern,
        name="inproj",
        grid=(m // tm, wp // tn),
        in_specs=[pl.BlockSpec((tm, d), lambda i, j: (i, 0)),
                  pl.BlockSpec((1, 1, 1, d), lambda i, j: (i // tiles_per_mod, 0, 0, 0)),
                  pl.BlockSpec((1, 1, 1, d), lambda i, j: (i // tiles_per_mod, 0, 0, 1)),
                  pl.BlockSpec((1, d), lambda i, j: (0, 0)),
                  pl.BlockSpec((None, d, tn), lambda i, j: (layer, 0, j)),
                  pl.BlockSpec((8, tn), lambda i, j: (0, j)),
                  pl.BlockSpec((1, tn), lambda i, j: (0, j))],
        out_specs=pl.BlockSpec((tm, tn), lambda i, j: (i, j)),
        out_shape=jax.ShapeDtypeStruct((m, wp), F32),
        scratch_shapes=[pltpu.VMEM((tm, d), BF16)],
        compiler_params=_params("parallel", "arbitrary"),
    )(h2d, shift, scale, pre_w, w_all, cw, cb)


def _direction_masks(d):
    ri = lax.broadcasted_iota(jnp.int32, (CHUNK, CHUNK), 0)
    ci = lax.broadcasted_iota(jnp.int32, (CHUNK, CHUNK), 1)
    diff = (ri - ci) * (1 - 2 * d)
    return diff >= 0, diff > 0, ri, ci


def _chunk_index(d, p, nc):
    return p + d * (nc - 1 - 2 * p)


def _ssd_chunks(xs_ref, bc_ref, sm_ref, par_ref, dsk_ref, eq_ref, y_ref, hT_ref, backward):
    heads_per_group = xs_ref.shape[-1] // (SSD_GROUPS * SSD_HEAD_DIM)
    gw = heads_per_group * SSD_HEAD_DIM
    ri = lax.broadcasted_iota(jnp.int32, (CHUNK, CHUNK), 0)
    ci = lax.broadcasted_iota(jnp.int32, (CHUNK, CHUNK), 1)
    incl = (ri <= ci) if backward else (ri >= ci)
    tri = incl.astype(F32)
    last = 0 if backward else CHUNK - 1
    par = par_ref[0]
    cps = xs_ref.shape[1] // CHUNK
    rows = [((cps - 1 - s) if backward else s) * CHUNK for s in range(cps)]
    groups = range(SSD_GROUPS)
    ppg = gw // LANE
    pairs = range(SSD_GROUPS * ppg)
    left = lax.broadcasted_iota(jnp.int32, (CHUNK, LANE), 1) < SSD_HEAD_DIM

    cg, xp, yd, eacum_x, cdec, upd = [], [], [], [], [], []
    for s in range(cps):
        sm = sm_ref[0, rows[s]:rows[s] + CHUNK, :]
        xs = xs_ref[0, rows[s]:rows[s] + CHUNK, :]
        bc = bc_ref[0, rows[s]:rows[s] + CHUNK, :]
        dt = _softplus(sm + par[0:1])
        loga = dt * par[1:2]
        acum = jnp.dot(tri, loga, precision=HIGHEST, preferred_element_type=F32)
        dt_t = dt.T
        hi = acum.astype(BF16)
        lo = (acum - hi.astype(F32)).astype(BF16)
        acum_q = _dot(hi, eq_ref[...]) + _dot(lo, eq_ref[...])
        acum_t = (hi.astype(F32) + lo.astype(F32)).T
        bg = [bc[:, g * SSD_STATE:(g + 1) * SSD_STATE].astype(BF16) for g in groups]
        cgs = [bc[:, (SSD_GROUPS + g) * SSD_STATE:(SSD_GROUPS + g + 1) * SSD_STATE].astype(BF16) for g in groups]
        cb = [_dot_nt(cgs[g], bg[g]) for g in groups]

        def intra(hh):
            seg = acum_q[:, hh * CHUNK:(hh + 1) * CHUNK] - acum_t[hh:hh + 1, :]
            return (cb[hh // heads_per_group] * jnp.where(incl, jnp.exp(seg), 0.0)
                    * dt_t[hh:hh + 1, :]).astype(BF16)

        xps = [xs[:, i * LANE:(i + 1) * LANE] for i in pairs]
        m_pair = [jnp.concatenate([intra(2 * i), intra(2 * i + 1)], axis=1) for i in pairs]
        x_bd = [jnp.concatenate([jnp.where(left, x, 0.0), jnp.where(left, 0.0, x)], axis=0).astype(BF16)
                for x in xps]
        yd.append([_dot(m_pair[i], x_bd[i]) for i in pairs])
        acum_x = [jnp.where(left, acum_q[:, 2 * i * CHUNK:2 * i * CHUNK + LANE],
                            acum_q[:, (2 * i + 1) * CHUNK:(2 * i + 1) * CHUNK + LANE]) for i in pairs]
        alast_x = [a[last:last + 1] for a in acum_x]
        dt_x = [jnp.where(left, dt[:, 2 * i:2 * i + 1], dt[:, 2 * i + 1:2 * i + 2]) for i in pairs]
        xw = [(xps[i] * (dt_x[i] * jnp.exp(alast_x[i] - acum_x[i]))).astype(BF16) for i in pairs]
        upd.append([_dot_tn(bg[g], jnp.concatenate(xw[g * ppg:(g + 1) * ppg], axis=1)) for g in groups])
        cdec.append([jnp.exp(jnp.concatenate(alast_x[g * ppg:(g + 1) * ppg], axis=1)) for g in groups])
        eacum_x.append([jnp.exp(a) for a in acum_x])
        cg.append(cgs)
        xp.append(xps)

    h_t = [hT_ref[0, 0, g] for g in groups]
    for s in range(cps):
        yoff = [_dot(cg[s][g], h_t[g].astype(BF16)) for g in groups]
        h_t = [h_t[g] * cdec[s][g] + upd[s][g] for g in groups]
        for i in pairs:
            lo = i * LANE
            yo = yoff[i // ppg][:, (i % ppg) * LANE:(i % ppg + 1) * LANE]
            y = yd[s][i] + yo * eacum_x[s][i]
            if not backward:
                y = y + xp[s][i] * dsk_ref[:, lo:lo + LANE]
            y_ref[0, 0, rows[s]:rows[s] + CHUNK, lo:lo + LANE] = y.astype(y_ref.dtype)
    for g in groups:
        hT_ref[0, 0, g] = h_t[g]


def _diag_block_inverses(a_list, xor_ij, block):
    eye = jnp.where(xor_ij == 0, 1.0, 0.0)
    in_pair = xor_ij < 2
    t_list = [eye - jnp.where(in_pair, a, 0.0) for a in a_list]
    size = 2
    while size < block:
        couples = (xor_ij >> (size.bit_length() - 1)) == 1
        a_off = [jnp.where(couples, a, 0.0).astype(BF16) for a in a_list]
        t16 = [t.astype(BF16) for t in t_list]
        ta = [_dot(t, a).astype(BF16) for t, a in zip(t16, a_off)]
        t_list = [t - _dot(x, tb) for t, x, tb in zip(t_list, ta, t16)]
        size *= 2
    return t_list


def _block_substitution(a_list, t_list, rhs_list, block, backward):
    n = a_list[0].shape[0]
    width = rhs_list[0].shape[1]
    order = range(n // block)
    order = list(reversed(order)) if backward else list(order)
    a16 = [a.astype(BF16) for a in a_list]
    t16 = [t.astype(BF16) for t in t_list]

    def place(y, i):
        parts = []
        if i > 0:
            parts.append(jnp.zeros((i * block, width), y.dtype))
        parts.append(y)
        if (i + 1) * block < n:
            parts.append(jnp.zeros((n - (i + 1) * block, width), y.dtype))
        return jnp.concatenate(parts, axis=0) if len(parts) > 1 else y

    x_list = [None] * len(a_list)
    for step, i in enumerate(order):
        rows = slice(i * block, (i + 1) * block)
        y = [r[rows] for r in rhs_list]
        if step > 0:
            y = [yy - _dot(a[rows], x.astype(BF16)) for yy, a, x in zip(y, a16, x_list)]
        xi = [_dot(t[rows], place(yy.astype(BF16), i)) for t, yy in zip(t16, y)]
        x_list = [place(v, i) if x is None else x + place(v, i) for x, v in zip(x_list, xi)]
    return x_list


def _gdn_chunks(q_ref, k_ref, v_ref, sm_ref, par_ref, o_ref, s_ref, is_backward, *, beta_lane, g_lane):
    heads = range(q_ref.shape[-1] // GDN_HEAD_DIM)
    par = par_ref[0]
    kk = GDN_HEAD_DIM
    cps = q_ref.shape[1] // CHUNK
    units = [(s, h) for s in range(cps) for h in heads]
    ri = lax.broadcasted_iota(jnp.int32, (CHUNK, CHUNK), 0)
    ci = lax.broadcasted_iota(jnp.int32, (CHUNK, CHUNK), 1)
    xor_ij = ri ^ ci

    def col(x, lane):
        return x[:, lane:lane + 1]

    def scan(backward):
        incl = (ri <= ci) if backward else (ri >= ci)
        strict = (ri < ci) if backward else (ri > ci)
        tri = incl.astype(F32)
        last = 0 if backward else CHUNK - 1
        rows = [((cps - 1 - s) if backward else s) * CHUNK for s in range(cps)]

        def head_cols(ref, s, h):
            return ref[0, rows[s]:rows[s] + CHUNK, h * kk:(h + 1) * kk]

        eg, eend, cdec, beta_all, gcum, gcum_t = [], [], [], [], [], []
        for s in range(cps):
            sm = sm_ref[0, rows[s]:rows[s] + CHUNK, :]
            beta_all.append(_sigmoid(sm))
            g_all = par[1:2] * _softplus(sm + par[0:1])
            gc = jnp.dot(tri, g_all, precision=HIGHEST, preferred_element_type=F32)
            glast = gc[last:last + 1]
            gcum.append(gc)
            gcum_t.append(gc.T)
            eg.append(jnp.exp(gc))
            eend.append(jnp.exp(glast - gc))
            cdec.append(jnp.exp(glast))
        qn, kn = {}, {}
        for s, h in units:
            q = head_cols(q_ref, s, h)
            k = head_cols(k_ref, s, h)
            qn[s, h] = q * (lax.rsqrt(jnp.sum(q * q, axis=-1, keepdims=True) + EPS) * kk ** -0.5)
            kn[s, h] = k * lax.rsqrt(jnp.sum(k * k, axis=-1, keepdims=True) + EPS)
        beta = {(s, h): col(beta_all[s], beta_lane + h) for s, h in units}
        dec = {(s, h): jnp.where(
            incl, jnp.exp(col(gcum[s], g_lane + h) - gcum_t[s][g_lane + h:g_lane + h + 1, :]), 0.0)
            for s, h in units}
        kb = {x: kn[x] * beta[x] for x in units}
        k16 = {x: kn[x].astype(BF16) for x in units}
        a = [jnp.where(strict, _dot_nt(kb[x].astype(BF16), k16[x]) * dec[x], 0.0) for x in units]
        attn = {x: (_dot_nt(qn[x].astype(BF16), k16[x]) * dec[x]).astype(BF16) for x in units}
        rhs = [jnp.concatenate([head_cols(v_ref, s, h) * beta[s, h], kb[s, h] * col(eg[s], g_lane + h)], axis=1)
               for s, h in units]
        t = _diag_block_inverses(a, xor_ij, GDN_SOLVE_BLOCK)
        sol = dict(zip(units, _block_substitution(a, t, rhs, GDN_SOLVE_BLOCK, backward)))
        wq = {(s, h): jnp.concatenate([sol[s, h][:, kk:].astype(BF16),
                                       (qn[s, h] * col(eg[s], g_lane + h)).astype(BF16)], axis=0)
              for s, h in units}
        ke = {(s, h): (kn[s, h] * col(eend[s], g_lane + h)).astype(BF16) for s, h in units}

        state = [s_ref[0, 0, h] for h in heads]
        for s in range(cps):
            ws = [_dot(wq[s, h], state[h].astype(BF16)) for h in heads]
            vnew = [(sol[s, h][:, :kk] - ws[h][:CHUNK]).astype(BF16) for h in heads]
            o = [ws[h][CHUNK:] + _dot(attn[s, h], vnew[h]) for h in heads]
            state = [state[h] * col(cdec[s], g_lane + h) + _dot_tn(ke[s, h], vnew[h]) for h in heads]
            for h in heads:
                o_ref[0, 0, rows[s]:rows[s] + CHUNK, h * kk:(h + 1) * kk] = o[h].astype(o_ref.dtype)
        for h in heads:
            s_ref[0, 0, h] = state[h]

    scan(is_backward)


def _scan_kernel(xs_ref, bc_ref, q_ref, k_ref, v_ref, sm_ref, spar_ref, gpar_ref, dsk_ref, eq_ref,
                 h0_ref, s0_ref, y_ref, hT_ref, o_ref, s_ref, *, beta_lane, g_lane):
    d = pl.program_id(1)
    p = pl.program_id(2)

    @pl.when(p == 0)
    def _():
        hT_ref[...] = h0_ref[...]
        s_ref[...] = s0_ref[...]

    def scans(backward):
        _gdn_chunks(q_ref, k_ref, v_ref, sm_ref, gpar_ref, o_ref, s_ref, backward,
                    beta_lane=beta_lane, g_lane=g_lane)
        _ssd_chunks(xs_ref, bc_ref, sm_ref, spar_ref, dsk_ref, eq_ref, y_ref, hT_ref, backward)

    @pl.when(d == 0)
    def _():
        scans(False)

    @pl.when(d == 1)
    def _():
        scans(True)


def _scans(p3d, ssd_par, gdn_par, dsk, e_q, h0, s0, *, cols, beta_lane, g_lane):
    bsz, length, _ = p3d.shape
    nc = length // CHUNK
    c_xs, c_qkv, c_bc, c_sm, w_xs, w_q, w_bc = cols
    cps = min(CHUNKS_PER_STEP, nc)
    nb = nc // cps
    rows = cps * CHUNK
    chunk = functools.partial(_chunk_index, nc=nb)

    def rows_spec(width, col_block):
        return pl.BlockSpec((1, rows, width), lambda b, d, p: (b, chunk(d, p), col_block))

    def state_spec(shape):
        return pl.BlockSpec((1, 1) + shape[2:], lambda b, d, p: (b, d) + (0,) * (len(shape) - 2))

    def out_spec(width):
        return pl.BlockSpec((1, 1, rows, width), lambda b, d, p: (d, b, chunk(d, p), 0))

    return pl.pallas_call(
        functools.partial(_scan_kernel, beta_lane=beta_lane, g_lane=g_lane),
        name="scans",
        grid=(bsz, N_DIR, nb),
        in_specs=[rows_spec(w_xs, c_xs // w_xs), rows_spec(w_bc, c_bc // w_bc)]
        + [rows_spec(w_q, c_qkv // w_q + i) for i in range(3)]
        + [pl.BlockSpec((1, rows, LANE), lambda b, d, p: (b, chunk(d, p), c_sm // LANE + d)),
           pl.BlockSpec((1, 8, LANE), lambda b, d, p: (d, 0, 0)),
           pl.BlockSpec((1, 8, LANE), lambda b, d, p: (d, 0, 0)),
           pl.BlockSpec((1, w_xs), lambda b, d, p: (0, 0)),
           pl.BlockSpec(e_q.shape, lambda b, d, p: (0, 0)),
           state_spec(h0.shape), state_spec(s0.shape)],
        out_specs=[out_spec(w_xs), state_spec(h0.shape), out_spec(w_q), state_spec(s0.shape)],
        out_shape=[jax.ShapeDtypeStruct((N_DIR, bsz, length, w_xs), SCAN_OUT_DTYPE),
                   jax.ShapeDtypeStruct(h0.shape, F32),
                   jax.ShapeDtypeStruct((N_DIR, bsz, length, w_q), SCAN_OUT_DTYPE),
                   jax.ShapeDtypeStruct(s0.shape, F32)],
        compiler_params=_params("parallel", "parallel", "arbitrary"),
    )(p3d, p3d, p3d, p3d, p3d, p3d, ssd_par, gdn_par, dsk, e_q, h0, s0)


def _outproj_kernel(y0_ref, y1_ref, o0_ref, o1_ref, z_ref, h_ref, gate_ref, w_ref,
                    snw_ref, gnw_ref, postw_ref, out_ref):
    zs = _silu(z_ref[...])
    ssd_w = y0_ref.shape[-1]
    gw = ssd_w // SSD_GROUPS
    ys = (y0_ref[0].astype(F32) + y1_ref[0].astype(F32)) * zs[:, :ssd_w]
    parts = []
    for g in range(SSD_GROUPS):
        yg = ys[:, g * gw:(g + 1) * gw]
        parts.append(yg * lax.rsqrt(jnp.mean(yg * yg, axis=-1, keepdims=True) + EPS))
    ysn = jnp.concatenate(parts, axis=1) * snw_ref[...]
    og = o0_ref[0].astype(F32) + o1_ref[0].astype(F32)
    parts = []
    for h in range(og.shape[-1] // GDN_HEAD_DIM):
        oh = og[:, h * GDN_HEAD_DIM:(h + 1) * GDN_HEAD_DIM]
        parts.append(oh * lax.rsqrt(jnp.mean(oh * oh, axis=-1, keepdims=True) + EPS))
    ogn = jnp.concatenate(parts, axis=1) * gnw_ref[...] * zs[:, ssd_w:]
    ycat = jnp.concatenate([ysn, ogn], axis=1).astype(BF16)
    m = _dot(ycat, w_ref[...])
    mn = m * lax.rsqrt(jnp.mean(m * m, axis=-1, keepdims=True) + EPS) * postw_ref[...]
    out_ref[...] = h_ref[...] + gate_ref[0, 0] * mn


def _outproj(y, o, p2d, h2d, gate, w_all, layer, snw, gnw, postw, *, rows_per_mod, tm):
    m, d = h2d.shape
    ssd_w = y.shape[-1]
    gdn_w = o.shape[-1]
    tiles_per_mod = rows_per_mod // tm
    y2 = y.reshape(N_DIR, m, ssd_w)
    o2 = o.reshape(N_DIR, m, gdn_w)
    return pl.pallas_call(
        _outproj_kernel,
        name="outproj",
        grid=(m // tm,),
        in_specs=[pl.BlockSpec((1, tm, ssd_w), lambda i: (0, i, 0)),
                  pl.BlockSpec((1, tm, ssd_w), lambda i: (1, i, 0)),
                  pl.BlockSpec((1, tm, gdn_w), lambda i: (0, i, 0)),
                  pl.BlockSpec((1, tm, gdn_w), lambda i: (1, i, 0)),
                  pl.BlockSpec((tm, ssd_w + gdn_w), lambda i: (i, 0)),
                  pl.BlockSpec((tm, d), lambda i: (i, 0)),
                  pl.BlockSpec((1, 1, 1, d), lambda i: (i // tiles_per_mod, 0, 0, 2)),
                  pl.BlockSpec((None, ssd_w + gdn_w, d), lambda i: (layer, 0, 0)),
                  pl.BlockSpec((1, ssd_w), lambda i: (0, 0)),
                  pl.BlockSpec((1, gdn_w), lambda i: (0, 0)),
                  pl.BlockSpec((1, d), lambda i: (0, 0))],
        out_specs=pl.BlockSpec((tm, d), lambda i: (i, 0)),
        out_shape=jax.ShapeDtypeStruct((m, d), F32),
        compiler_params=_params("parallel"),
    )(y2, y2, o2, o2, p2d, h2d, gate, w_all, snw, gnw, postw)


def kernel(x, c, ctx, c_ctx, w_ada, b_ada, pre_norm_w, post_norm_w, w_in, conv_ssd_w, conv_ssd_b,
           conv_gdn_w, ssd_a_log, ssd_dt_bias, ssd_d, ssd_norm_w, gdn_a_log, gdn_dt_bias,
           gdn_norm_w, w_out):
    bsz, seq, d = x.shape
    ctx_len = ctx.shape[1]
    depth = w_in.shape[0]
    d_mix = w_out.shape[1]
    ssd_w = d_mix // 2
    gdn_w = d_mix - ssd_w
    ssd_heads = ssd_w // SSD_HEAD_DIM
    gdn_heads = gdn_w // GDN_HEAD_DIM
    bc_w = 2 * SSD_GROUPS * SSD_STATE
    assert ssd_w == gdn_w and ssd_heads + 2 * gdn_heads <= LANE
    assert seq % CHUNK == 0 and ctx_len % CHUNK == 0 and CHUNK % GRID_W == 0

    c_z, c_xs = 0, d_mix
    c_qkv = c_xs + ssd_w
    c_bc = c_qkv + 3 * gdn_w
    c_sm = c_bc + bc_w
    tn = 1024
    wp = -(-(c_sm + N_DIR * LANE) // tn) * tn
    assert c_qkv % gdn_w == 0 and c_xs % ssd_w == 0 and c_bc % bc_w == 0 and c_sm % LANE == 0
    assert c_xs % tn == 0 and c_sm // tn == wp // tn - 1
    beta_lane, g_lane = ssd_heads, ssd_heads + gdn_heads

    o_xbc = d_mix
    o_dt = o_xbc + ssd_w + bc_w
    o_qkv = o_dt + N_DIR * ssd_heads
    o_beta = o_qkv + 3 * gdn_w
    o_alpha = o_beta + N_DIR * gdn_heads

    assert w_in.shape[-1] == o_alpha + N_DIR * gdn_heads
    w_cat = _pack_weights(w_in, wp, d_mix=d_mix, ssd_w=ssd_w, bc_w=bc_w, qkv_w=3 * gdn_w,
                          ssd_heads=ssd_heads, gdn_heads=gdn_heads)

    def conv_layout(ssd_part, gdn_part):
        lead = ssd_part.shape[:-1]
        return jnp.concatenate(
            [jnp.zeros(lead + (d_mix,), F32), ssd_part[..., :ssd_w], gdn_part, ssd_part[..., ssd_w:],
             jnp.zeros(lead + (wp - c_sm,), F32)], axis=-1)

    cw_all = conv_layout(conv_ssd_w.astype(F32), conv_gdn_w.astype(F32))
    cw_all = jnp.concatenate([cw_all, jnp.zeros((depth, 8 - CONV_K, wp), F32)], axis=1)
    cb_all = conv_layout(conv_ssd_b.astype(F32)[:, None], jnp.zeros((depth, 1, 3 * gdn_w), F32))

    def lane_row(vals, lane0):
        return jnp.pad(vals, ((0, 0), (0, 0), (lane0, LANE - lane0 - vals.shape[-1])))

    zeros_rows = jnp.zeros((depth, N_DIR, 6, LANE), F32)
    ssd_par = jnp.concatenate([lane_row(ssd_dt_bias.astype(F32), 0)[:, :, None],
                               lane_row(-jnp.exp(ssd_a_log.astype(F32)), 0)[:, :, None], zeros_rows], axis=2)
    gdn_par = jnp.concatenate([lane_row(gdn_dt_bias.astype(F32), g_lane)[:, :, None],
                               lane_row(-jnp.exp(gdn_a_log.astype(F32)), g_lane)[:, :, None], zeros_rows], axis=2)
    dsk_all = jnp.repeat(ssd_d.astype(F32), SSD_HEAD_DIM, axis=-1)[:, None]
    gnw_all = jnp.tile(gdn_norm_w.astype(F32), (1, gdn_heads))[:, None]
    w_out16 = w_out.astype(BF16)
    head_of_lane = jnp.arange(LANE)[:, None]
    e_q = (jnp.arange(ssd_heads * CHUNK)[None] // CHUNK == head_of_lane).astype(BF16)

    rows = 16
    c_all = jnp.concatenate([c, c_ctx[None], jnp.zeros((rows - bsz - 1, d), c.dtype)], axis=0)
    mod = _modulation(c_all, w_ada, b_ada).reshape(depth, rows, 1, 3 * d)

    h_lat = x.reshape(bsz * seq, d)
    h_ctx = ctx.reshape(bsz * ctx_len, d)
    tm_lat = min(1024, seq)
    tm_ctx = min(1024, bsz * ctx_len)
    tm_out = 256
    conv_cols = (c_xs, c_sm)
    scan_cols = (c_xs, c_qkv, c_bc, c_sm, ssd_w, gdn_w, bc_w)
    ssd_state0 = jnp.zeros((bsz, N_DIR, SSD_GROUPS, SSD_STATE, ssd_w // SSD_GROUPS), F32)
    gdn_state0 = jnp.zeros((bsz, N_DIR, gdn_heads, GDN_HEAD_DIM, GDN_HEAD_DIM), F32)

    for l in range(depth):
        mod_lat = mod[l, :bsz, None]
        mod_ctx = mod[l, bsz:bsz + 1, None]
        pre_w = pre_norm_w[l][None]
        p_lat = _inproj(h_lat, mod_lat, mod_lat, pre_w, w_cat, l, cw_all[l], cb_all[l],
                        rows_per_mod=seq, group=GRID_W, tm=tm_lat, tn=tn, conv_cols=conv_cols)
        p_ctx = _inproj(h_ctx, mod_ctx, mod_ctx, pre_w, w_cat, l, cw_all[l], cb_all[l],
                        rows_per_mod=bsz * ctx_len, group=ctx_len, tm=tm_ctx, tn=tn, conv_cols=conv_cols)
        p_lat3 = p_lat.reshape(bsz, seq, wp)
        p_ctx3 = p_ctx.reshape(bsz, ctx_len, wp)

        y_ctx, ssd_state, o_ctx, gdn_state = _scans(
            p_ctx3, ssd_par[l], gdn_par[l], dsk_all[l], e_q, ssd_state0, gdn_state0,
            cols=scan_cols, beta_lane=beta_lane, g_lane=g_lane)
        y_lat, _, o_lat, _ = _scans(
            p_lat3, ssd_par[l], gdn_par[l], dsk_all[l], e_q, ssd_state, gdn_state,
            cols=scan_cols, beta_lane=beta_lane, g_lane=g_lane)

        snw = ssd_norm_w[l][None].astype(F32)
        post_w = post_norm_w[l][None]
        h_lat_new = _outproj(y_lat, o_lat, p_lat, h_lat, mod_lat, w_out16, l, snw, gnw_all[l], post_w,
                             rows_per_mod=seq, tm=min(tm_out, seq))
        if l < depth - 1:
            h_ctx = _outproj(y_ctx, o_ctx, p_ctx, h_ctx, mod_ctx, w_out16, l, snw, gnw_all[l], post_w,
                             rows_per_mod=bsz * ctx_len, tm=min(tm_out, bsz * ctx_len))
        h_lat = h_lat_new
    return h_lat.reshape(bsz, seq, d)
```

```python
import functools

import jax
import jax.numpy as jnp
from jax import lax
from jax.experimental import pallas as pl
from jax.experimental.pallas import tpu as pltpu

F32 = jnp.float32
BF16 = jnp.bfloat16
HIGHEST = lax.Precision.HIGHEST

EPS = 1e-6
CHUNK = 128
GRID_W = 64
CONV_K = 5
N_DIR = 2
SSD_HEAD_DIM = 64
SSD_STATE = 128
SSD_GROUPS = 2
GDN_HEAD_DIM = 128
LANE = 128
SUBLANE = 8
NORM_UNROLL = 8
SCAN_OUT_DTYPE = jnp.bfloat16
CHUNKS_PER_STEP = 4
GDN_SOLVE_BLOCK = 32
SUB_ROWS = 256
NORM_ROWS = 16
VMEM_LIMIT = 52 * 1024 * 1024

NT = (((1,), (1,)), ((), ()))
TN = (((0,), (0,)), ((), ()))


def _dot(a, b):
    return jnp.dot(a, b, preferred_element_type=F32)


def _dot_nt(a, b):
    return lax.dot_general(a, b, NT, preferred_element_type=F32)


def _dot_tn(a, b):
    return lax.dot_general(a, b, TN, preferred_element_type=F32)


def _sigmoid(x):
    return 1.0 / (1.0 + jnp.exp(-x))


def _silu(x):
    half = 0.5 * x
    return half + half * jnp.tanh(half)


def _softplus(x):
    return jnp.maximum(x, 0.0) + jnp.log(1.0 + jnp.exp(-jnp.abs(x)))


def _params(*sem, flags=None):
    return pltpu.CompilerParams(dimension_semantics=sem, vmem_limit_bytes=VMEM_LIMIT, flags=flags)


def _mod_kernel(c_ref, w_ref, b_ref, o_ref):
    a = _silu(c_ref[...])
    o_ref[0] = jnp.dot(a, w_ref[0], precision=HIGHEST, preferred_element_type=F32) + b_ref[0]


def _modulation(c_all, w_ada, b_ada):
    depth, d, n = w_ada.shape
    rows = c_all.shape[0]
    tn = 768
    return pl.pallas_call(
        _mod_kernel,
        name="adaln_mod",
        grid=(depth, n // tn),
        in_specs=[pl.BlockSpec((rows, d), lambda l, j: (0, 0)),
                  pl.BlockSpec((1, d, tn), lambda l, j: (l, 0, j)),
                  pl.BlockSpec((1, 1, tn), lambda l, j: (l, 0, j))],
        out_specs=pl.BlockSpec((1, rows, tn), lambda l, j: (l, 0, j)),
        out_shape=jax.ShapeDtypeStruct((depth, rows, n), F32),
        compiler_params=_params("parallel", "parallel"),
    )(c_all, w_ada, b_ada.reshape(depth, 1, n))


def _pack_kernel(w_ref, o_ref, *, d_mix, ssd_w, bc_w, qkv_w, ssd_heads, gdn_heads):
    half = d_mix + ssd_w + bc_w
    a = w_ref[0, :, :half]
    b = w_ref[0, :, half:]
    rows = a.shape[0]
    n_dt, n_g = N_DIR * ssd_heads, N_DIR * gdn_heads
    c_qkv = d_mix + ssd_w
    c_bc = c_qkv + qkv_w
    c_sm = c_bc + bc_w
    o_ref[0, :, 0:c_qkv] = a[:, 0:c_qkv].astype(BF16)
    o_ref[0, :, c_qkv:c_bc] = b[:, n_dt:n_dt + qkv_w].astype(BF16)
    o_ref[0, :, c_bc:c_sm] = a[:, c_qkv:c_qkv + bc_w].astype(BF16)
    for dd in range(N_DIR):
        beta0 = n_dt + qkv_w + dd * gdn_heads
        alpha0 = beta0 + n_g
        small = jnp.concatenate(
            [b[:, dd * ssd_heads:(dd + 1) * ssd_heads], b[:, beta0:beta0 + gdn_heads],
             b[:, alpha0:alpha0 + gdn_heads], jnp.zeros((rows, LANE - ssd_heads - 2 * gdn_heads), F32)], axis=1)
        o_ref[0, :, c_sm + dd * LANE:c_sm + (dd + 1) * LANE] = small.astype(BF16)
    pad0 = c_sm + N_DIR * LANE
    o_ref[0, :, pad0:] = jnp.zeros((rows, o_ref.shape[-1] - pad0), BF16)


def _pack_weights(w_in, wp, **dims):
    depth, d, in_dim = w_in.shape
    assert (dims["d_mix"] + dims["ssd_w"] + dims["bc_w"]) % LANE == 0
    rows = 256
    return pl.pallas_call(
        functools.partial(_pack_kernel, **dims),
        name="pack_w_in",
        grid=(depth, d // rows),
        in_specs=[pl.BlockSpec((1, rows, in_dim), lambda l, r: (l, r, 0))],
        out_specs=pl.BlockSpec((1, rows, wp), lambda l, r: (l, r, 0)),
        out_shape=jax.ShapeDtypeStruct((depth, d, wp), BF16),
        compiler_params=_params("parallel", "parallel"),
    )(w_in)


def _inproj_kernel(h_ref, shift_ref, scale_ref, prew_ref, w_ref, cw_ref, cb_ref, o_ref, u_ref,
                   *, group, j_lo, j_hi, mixed_lanes):
    j = pl.program_id(1)
    tm, tn = o_ref.shape
    sub = max(group, min(tm, SUB_ROWS))

    @pl.when(j == 0)
    def _():
        prew = prew_ref[...]
        scale1p = 1.0 + scale_ref[0, 0]
        shift = shift_ref[0, 0]

        def norm_strip(s, carry):
            r0 = pl.multiple_of(s * NORM_ROWS, NORM_ROWS)
            x = h_ref[pl.ds(r0, NORM_ROWS), :]
            y = x * lax.rsqrt(jnp.mean(x * x, axis=-1, keepdims=True) + EPS) * prew
            u_ref[pl.ds(r0, NORM_ROWS), :] = (y * scale1p + shift).astype(BF16)
            return carry

        lax.fori_loop(0, tm // NORM_ROWS, norm_strip, 0, unroll=NORM_UNROLL)

    def conv_tile(lanes):
        cw = cw_ref[:, :lanes]
        bias = cb_ref[:, :lanes]
        row = lax.broadcasted_iota(jnp.int32, (SUBLANE, lanes), 0)
        offs = [k - CONV_K // 2 for k in range(CONV_K)]
        w_mid = [cw[k:k + 1] for k in range(CONV_K)]
        w_first = [jnp.where(row >= -off, w, 0.0) if off < 0 else w for off, w in zip(offs, w_mid)]
        w_last = [jnp.where(row < SUBLANE - off, w, 0.0) if off > 0 else w for off, w in zip(offs, w_mid)]
        last0 = group - SUBLANE

        def taps(pieces, weights, lo, hi):
            out = bias
            for x, w in zip(pieces, weights):
                out = out + x[lo:hi] * w
            return _silu(out)

        n_sub = tm // sub
        acc_next = _dot(u_ref[0:sub, :], w_ref[...])
        for sb in range(n_sub):
            acc = acc_next
            if sb + 1 < n_sub:
                acc_next = _dot(u_ref[(sb + 1) * sub:(sb + 2) * sub, :], w_ref[...])
            if lanes < tn:
                o_ref[sb * sub:(sb + 1) * sub, lanes:] = acc[:, lanes:]
            a = acc[:, :lanes]
            shifted = [a if off == 0 else pltpu.roll(a, (-off) % sub, axis=0) for off in offs]
            for gi in range(sub // group):
                g0 = gi * group
                o0 = sb * sub + g0
                o_ref[o0:o0 + SUBLANE, :lanes] = taps(shifted, w_first, g0, g0 + SUBLANE)
                o_ref[o0 + SUBLANE:o0 + last0, :lanes] = taps(shifted, w_mid, g0 + SUBLANE, g0 + last0)
                o_ref[o0 + last0:o0 + group, :lanes] = taps(shifted, w_last, g0 + last0, g0 + group)

    @pl.when(j < j_lo)
    def _():
        o_ref[...] = _silu(_dot(u_ref[...], w_ref[...]))

    @pl.when((j >= j_lo) & (j < j_hi))
    def _():
        conv_tile(tn)

    @pl.when(j >= j_hi)
    def _():
        conv_tile(mixed_lanes)


def _inproj(h2d, shift, scale, pre_w, w_all, layer, cw, cb, *, rows_per_mod, group, tm, tn, conv_cols):
    m, d = h2d.shape
    wp = w_all.shape[-1]
    tiles_per_mod = rows_per_mod // tm
    kern = functools.partial(_inproj_kernel, group=group, j_lo=conv_cols[0] // tn, j_hi=conv_cols[1] // tn,
                             mixed_lanes=conv_cols[1] % tn)
    return pl.pallas_call(
        kern,
        name="inproj",
        grid=(m // tm, wp // tn),
        in_specs=[pl.BlockSpec((tm, d), lambda i, j: (i, 0)),
                  pl.BlockSpec((1, 1, 1, d), lambda i, j: (i // tiles_per_mod, 0, 0, 0)),
                  pl.BlockSpec((1, 1, 1, d), lambda i, j: (i // tiles_per_mod, 0, 0, 1)),
                  pl.BlockSpec((1, d), lambda i, j: (0, 0)),
                  pl.BlockSpec((None, d, tn), lambda i, j: (layer, 0, j)),
                  pl.BlockSpec((8, tn), lambda i, j: (0, j)),
                  pl.BlockSpec((1, tn), lambda i, j: (0, j))],
        out_specs=pl.BlockSpec((tm, tn), lambda i, j: (i, j)),
        out_shape=jax.ShapeDtypeStruct((m, wp), F32),
        scratch_shapes=[pltpu.VMEM((tm, d), BF16)],
        compiler_params=_params("parallel", "arbitrary"),
    )(h2d, shift, scale, pre_w, w_all, cw, cb)


def _chunk_index(d, p, nc):
    return p + d * (nc - 1 - 2 * p)


def _causal_masks(backward):
    ri = lax.broadcasted_iota(jnp.int32, (CHUNK, CHUNK), 0)
    ci = lax.broadcasted_iota(jnp.int32, (CHUNK, CHUNK), 1)
    return ((ri <= ci), (ri < ci)) if backward else ((ri >= ci), (ri > ci))


def _log_decay_cumsums(sm_ref, spar_ref, gpar_ref, rows, backward):
    tri = _causal_masks(backward)[0].astype(BF16)
    spar = spar_ref[0]
    gpar = gpar_ref[0]
    out = []
    for r in rows:
        sm = sm_ref[0, r:r + CHUNK, :]
        x = _softplus(sm + spar[0:1]) * spar[1:2] + _softplus(sm + gpar[0:1]) * gpar[1:2]
        hi = x.astype(BF16)
        rest = x - hi.astype(F32)
        mid = rest.astype(BF16)
        lo = (rest - mid.astype(F32)).astype(BF16)
        parts = _dot(tri, jnp.concatenate([hi, mid, lo], axis=1))
        out.append(parts[:, :LANE] + parts[:, LANE:2 * LANE] + parts[:, 2 * LANE:])
    return out


def _ssd_chunks(xs_ref, bc_ref, sm_ref, par_ref, dsk_ref, eq_ref, y_ref, hT_ref, cums, backward):
    heads_per_group = xs_ref.shape[-1] // (SSD_GROUPS * SSD_HEAD_DIM)
    gw = heads_per_group * SSD_HEAD_DIM
    incl = _causal_masks(backward)[0]
    last = 0 if backward else CHUNK - 1
    par = par_ref[0]
    cps = xs_ref.shape[1] // CHUNK
    rows = [((cps - 1 - s) if backward else s) * CHUNK for s in range(cps)]
    groups = range(SSD_GROUPS)
    ppg = gw // LANE
    pairs = range(SSD_GROUPS * ppg)
    left = lax.broadcasted_iota(jnp.int32, (CHUNK, LANE), 1) < SSD_HEAD_DIM

    cg, xp, yd, eacum_x, cdec, upd = [], [], [], [], [], []
    for s in range(cps):
        sm = sm_ref[0, rows[s]:rows[s] + CHUNK, :]
        xs = xs_ref[0, rows[s]:rows[s] + CHUNK, :]
        bc = bc_ref[0, rows[s]:rows[s] + CHUNK, :]
        dt = _softplus(sm + par[0:1])
        acum = cums[s]
        dt_t = dt.T
        hi = acum.astype(BF16)
        lo = (acum - hi.astype(F32)).astype(BF16)
        acum_q = _dot(hi, eq_ref[...]) + _dot(lo, eq_ref[...])
        acum_t = (hi.astype(F32) + lo.astype(F32)).T
        bg = [bc[:, g * SSD_STATE:(g + 1) * SSD_STATE].astype(BF16) for g in groups]
        cgs = [bc[:, (SSD_GROUPS + g) * SSD_STATE:(SSD_GROUPS + g + 1) * SSD_STATE].astype(BF16) for g in groups]
        cb = [_dot_nt(cgs[g], bg[g]) for g in groups]

        def intra(hh):
            seg = acum_q[:, hh * CHUNK:(hh + 1) * CHUNK] - acum_t[hh:hh + 1, :]
            return (cb[hh // heads_per_group] * jnp.where(incl, jnp.exp(seg), 0.0)
                    * dt_t[hh:hh + 1, :]).astype(BF16)

        xps = [xs[:, i * LANE:(i + 1) * LANE] for i in pairs]
        m_pair = [jnp.concatenate([intra(2 * i), intra(2 * i + 1)], axis=1) for i in pairs]
        x_bd = [jnp.concatenate([jnp.where(left, x, 0.0), jnp.where(left, 0.0, x)], axis=0).astype(BF16)
                for x in xps]
        yd.append([_dot(m_pair[i], x_bd[i]) for i in pairs])
        acum_x = [jnp.where(left, acum_q[:, 2 * i * CHUNK:2 * i * CHUNK + LANE],
                            acum_q[:, (2 * i + 1) * CHUNK:(2 * i + 1) * CHUNK + LANE]) for i in pairs]
        alast_x = [a[last:last + 1] for a in acum_x]
        dt_x = [jnp.where(left, dt[:, 2 * i:2 * i + 1], dt[:, 2 * i + 1:2 * i + 2]) for i in pairs]
        xw = [(xps[i] * (dt_x[i] * jnp.exp(alast_x[i] - acum_x[i]))).astype(BF16) for i in pairs]
        upd.append([_dot_tn(bg[g], jnp.concatenate(xw[g * ppg:(g + 1) * ppg], axis=1)) for g in groups])
        cdec.append([jnp.exp(jnp.concatenate(alast_x[g * ppg:(g + 1) * ppg], axis=1)) for g in groups])
        eacum_x.append([jnp.exp(a) for a in acum_x])
        cg.append(cgs)
        xp.append(xps)

    h_t = [hT_ref[0, 0, g] for g in groups]
    for s in range(cps):
        yoff = [_dot(cg[s][g], h_t[g].astype(BF16)) for g in groups]
        h_t = [h_t[g] * cdec[s][g] + upd[s][g] for g in groups]
        for i in pairs:
            lo = i * LANE
            yo = yoff[i // ppg][:, (i % ppg) * LANE:(i % ppg + 1) * LANE]
            y = yd[s][i] + yo * eacum_x[s][i]
            if not backward:
                y = y + xp[s][i] * dsk_ref[:, lo:lo + LANE]
            y_ref[0, 0, rows[s]:rows[s] + CHUNK, lo:lo + LANE] = y.astype(y_ref.dtype)
    for g in groups:
        hT_ref[0, 0, g] = h_t[g]


def _diag_block_inverses(a_list, xor_ij, block):
    eye = jnp.where(xor_ij == 0, 1.0, 0.0)
    in_pair = xor_ij < 2
    t_list = [eye - jnp.where(in_pair, a, 0.0) for a in a_list]
    size = 2
    while size < block:
        couples = (xor_ij >> (size.bit_length() - 1)) == 1
        a_off = [jnp.where(couples, a, 0.0).astype(BF16) for a in a_list]
        t16 = [t.astype(BF16) for t in t_list]
        ta = [_dot(t, a).astype(BF16) for t, a in zip(t16, a_off)]
        t_list = [t - _dot(x, tb) for t, x, tb in zip(t_list, ta, t16)]
        size *= 2
    return t_list


def _block_substitution(a_list, t_list, rhs_list, block, backward):
    n = a_list[0].shape[0]
    width = rhs_list[0].shape[1]
    order = range(n // block)
    order = list(reversed(order)) if backward else list(order)
    a16 = [a.astype(BF16) for a in a_list]
    t16 = [t.astype(BF16) for t in t_list]

    def place(y, i):
        parts = []
        if i > 0:
            parts.append(jnp.zeros((i * block, width), y.dtype))
        parts.append(y)
        if (i + 1) * block < n:
            parts.append(jnp.zeros((n - (i + 1) * block, width), y.dtype))
        return jnp.concatenate(parts, axis=0) if len(parts) > 1 else y

    x_list = [None] * len(a_list)
    for step, i in enumerate(order):
        rows = slice(i * block, (i + 1) * block)
        y = [r[rows] for r in rhs_list]
        if step > 0:
            y = [yy - _dot(a[rows], x.astype(BF16)) for yy, a, x in zip(y, a16, x_list)]
        xi = [_dot(t[rows], place(yy.astype(BF16), i)) for t, yy in zip(t16, y)]
        x_list = [place(v, i) if x is None else x + place(v, i) for x, v in zip(x_list, xi)]
    return x_list


def _gdn_chunks(q_ref, k_ref, v_ref, sm_ref, o_ref, s_ref, cums, is_backward, *, beta_lane, g_lane):
    heads = range(q_ref.shape[-1] // GDN_HEAD_DIM)
    kk = GDN_HEAD_DIM
    cps = q_ref.shape[1] // CHUNK
    units = [(s, h) for s in range(cps) for h in heads]
    ri = lax.broadcasted_iota(jnp.int32, (CHUNK, CHUNK), 0)
    ci = lax.broadcasted_iota(jnp.int32, (CHUNK, CHUNK), 1)
    xor_ij = ri ^ ci

    def col(x, lane):
        return x[:, lane:lane + 1]

    def scan(backward):
        incl, strict = _causal_masks(backward)
        last = 0 if backward else CHUNK - 1
        rows = [((cps - 1 - s) if backward else s) * CHUNK for s in range(cps)]

        def head_cols(ref, s, h):
            return ref[0, rows[s]:rows[s] + CHUNK, h * kk:(h + 1) * kk]

        eg, eend, cdec, beta_all, gcum, gcum_t = [], [], [], [], [], []
        for s in range(cps):
            sm = sm_ref[0, rows[s]:rows[s] + CHUNK, :]
            beta_all.append(_sigmoid(sm))
            gc = cums[s]
            glast = gc[last:last + 1]
            gcum.append(gc)
            gcum_t.append(gc.T)
            eg.append(jnp.exp(gc))
            eend.append(jnp.exp(glast - gc))
            cdec.append(jnp.exp(glast))
        qn, kn = {}, {}
        for s, h in units:
            q = head_cols(q_ref, s, h)
            k = head_cols(k_ref, s, h)
            qn[s, h] = q * (lax.rsqrt(jnp.sum(q * q, axis=-1, keepdims=True) + EPS) * kk ** -0.5)
            kn[s, h] = k * lax.rsqrt(jnp.sum(k * k, axis=-1, keepdims=True) + EPS)
        beta = {(s, h): col(beta_all[s], beta_lane + h) for s, h in units}
        dec = {(s, h): jnp.where(
            incl, jnp.exp(col(gcum[s], g_lane + h) - gcum_t[s][g_lane + h:g_lane + h + 1, :]), 0.0)
            for s, h in units}
        kb = {x: kn[x] * beta[x] for x in units}
        k16 = {x: kn[x].astype(BF16) for x in units}
        a = [jnp.where(strict, _dot_nt(kb[x].astype(BF16), k16[x]) * dec[x], 0.0) for x in units]
        attn = {x: (_dot_nt(qn[x].astype(BF16), k16[x]) * dec[x]).astype(BF16) for x in units}
        rhs = [jnp.concatenate([head_cols(v_ref, s, h) * beta[s, h], kb[s, h] * col(eg[s], g_lane + h)], axis=1)
               for s, h in units]
        t = _diag_block_inverses(a, xor_ij, GDN_SOLVE_BLOCK)
        sol = dict(zip(units, _block_substitution(a, t, rhs, GDN_SOLVE_BLOCK, backward)))
        wq = {(s, h): jnp.concatenate([sol[s, h][:, kk:].astype(BF16),
                                       (qn[s, h] * col(eg[s], g_lane + h)).astype(BF16)], axis=0)
              for s, h in units}
        ke = {(s, h): (kn[s, h] * col(eend[s], g_lane + h)).astype(BF16) for s, h in units}

        state = [s_ref[0, 0, h] for h in heads]
        for s in range(cps):
            ws = [_dot(wq[s, h], state[h].astype(BF16)) for h in heads]
            vnew = [(sol[s, h][:, :kk] - ws[h][:CHUNK]).astype(BF16) for h in heads]
            o = [ws[h][CHUNK:] + _dot(attn[s, h], vnew[h]) for h in heads]
            state = [state[h] * col(cdec[s], g_lane + h) + _dot_tn(ke[s, h], vnew[h]) for h in heads]
            for h in heads:
                o_ref[0, 0, rows[s]:rows[s] + CHUNK, h * kk:(h + 1) * kk] = o[h].astype(o_ref.dtype)
        for h in heads:
            s_ref[0, 0, h] = state[h]

    scan(is_backward)


def _scan_kernel(xs_ref, bc_ref, q_ref, k_ref, v_ref, sm_ref, spar_ref, gpar_ref, dsk_ref, eq_ref,
                 h0_ref, s0_ref, y_ref, hT_ref, o_ref, s_ref, *, beta_lane, g_lane):
    d = pl.program_id(1)
    p = pl.program_id(2)

    @pl.when(p == 0)
    def _():
        hT_ref[...] = h0_ref[...]
        s_ref[...] = s0_ref[...]

    def scans(backward):
        cps = sm_ref.shape[1] // CHUNK
        rows = [((cps - 1 - s) if backward else s) * CHUNK for s in range(cps)]
        cums = _log_decay_cumsums(sm_ref, spar_ref, gpar_ref, rows, backward)
        _gdn_chunks(q_ref, k_ref, v_ref, sm_ref, o_ref, s_ref, cums, backward,
                    beta_lane=beta_lane, g_lane=g_lane)
        _ssd_chunks(xs_ref, bc_ref, sm_ref, spar_ref, dsk_ref, eq_ref, y_ref, hT_ref, cums, backward)

    @pl.when(d == 0)
    def _():
        scans(False)

    @pl.when(d == 1)
    def _():
        scans(True)


def _scans(p3d, ssd_par, gdn_par, dsk, e_q, h0, s0, *, cols, beta_lane, g_lane):
    bsz, length, _ = p3d.shape
    nc = length // CHUNK
    c_xs, c_qkv, c_bc, c_sm, w_xs, w_q, w_bc = cols
    cps = min(CHUNKS_PER_STEP, nc)
    nb = nc // cps
    rows = cps * CHUNK
    chunk = functools.partial(_chunk_index, nc=nb)

    def rows_spec(width, col_block):
        return pl.BlockSpec((1, rows, width), lambda b, d, p: (b, chunk(d, p), col_block))

    def state_spec(shape):
        return pl.BlockSpec((1, 1) + shape[2:], lambda b, d, p: (b, d) + (0,) * (len(shape) - 2))

    def out_spec(width):
        return pl.BlockSpec((1, 1, rows, width), lambda b, d, p: (d, b, chunk(d, p), 0))

    return pl.pallas_call(
        functools.partial(_scan_kernel, beta_lane=beta_lane, g_lane=g_lane),
        name="scans",
        grid=(bsz, N_DIR, nb),
        in_specs=[rows_spec(w_xs, c_xs // w_xs), rows_spec(w_bc, c_bc // w_bc)]
        + [rows_spec(w_q, c_qkv // w_q + i) for i in range(3)]
        + [pl.BlockSpec((1, rows, LANE), lambda b, d, p: (b, chunk(d, p), c_sm // LANE + d)),
           pl.BlockSpec((1, 8, LANE), lambda b, d, p: (d, 0, 0)),
           pl.BlockSpec((1, 8, LANE), lambda b, d, p: (d, 0, 0)),
           pl.BlockSpec((1, w_xs), lambda b, d, p: (0, 0)),
           pl.BlockSpec(e_q.shape, lambda b, d, p: (0, 0)),
           state_spec(h0.shape), state_spec(s0.shape)],
        out_specs=[out_spec(w_xs), state_spec(h0.shape), out_spec(w_q), state_spec(s0.shape)],
        out_shape=[jax.ShapeDtypeStruct((N_DIR, bsz, length, w_xs), SCAN_OUT_DTYPE),
                   jax.ShapeDtypeStruct(h0.shape, F32),
                   jax.ShapeDtypeStruct((N_DIR, bsz, length, w_q), SCAN_OUT_DTYPE),
                   jax.ShapeDtypeStruct(s0.shape, F32)],
        compiler_params=_params("parallel", "parallel", "arbitrary"),
    )(p3d, p3d, p3d, p3d, p3d, p3d, ssd_par, gdn_par, dsk, e_q, h0, s0)


def _outproj_kernel(y0_ref, y1_ref, o0_ref, o1_ref, z_ref, h_ref, gate_ref, w_ref,
                    snw_ref, gnw_ref, postw_ref, out_ref):
    zs = z_ref[...]
    ssd_w = y0_ref.shape[-1]
    gw = ssd_w // SSD_GROUPS
    ys = (y0_ref[0].astype(F32) + y1_ref[0].astype(F32)) * zs[:, :ssd_w]
    parts = []
    for g in range(SSD_GROUPS):
        yg = ys[:, g * gw:(g + 1) * gw]
        parts.append(yg * lax.rsqrt(jnp.mean(yg * yg, axis=-1, keepdims=True) + EPS))
    ysn = jnp.concatenate(parts, axis=1) * snw_ref[...]
    og = o0_ref[0].astype(F32) + o1_ref[0].astype(F32)
    parts = []
    for h in range(og.shape[-1] // GDN_HEAD_DIM):
        oh = og[:, h * GDN_HEAD_DIM:(h + 1) * GDN_HEAD_DIM]
        parts.append(oh * lax.rsqrt(jnp.mean(oh * oh, axis=-1, keepdims=True) + EPS))
    ogn = jnp.concatenate(parts, axis=1) * gnw_ref[...] * zs[:, ssd_w:]
    ycat = jnp.concatenate([ysn, ogn], axis=1).astype(BF16)
    m = _dot(ycat, w_ref[...])
    mn = m * lax.rsqrt(jnp.mean(m * m, axis=-1, keepdims=True) + EPS) * postw_ref[...]
    out_ref[...] = h_ref[...] + gate_ref[0, 0] * mn


def _outproj(y, o, p2d, h2d, gate, w_all, layer, snw, gnw, postw, *, rows_per_mod, tm):
    m, d = h2d.shape
    ssd_w = y.shape[-1]
    gdn_w = o.shape[-1]
    tiles_per_mod = rows_per_mod // tm
    y2 = y.reshape(N_DIR, m, ssd_w)
    o2 = o.reshape(N_DIR, m, gdn_w)
    return pl.pallas_call(
        _outproj_kernel,
        name="outproj",
        grid=(m // tm,),
        in_specs=[pl.BlockSpec((1, tm, ssd_w), lambda i: (0, i, 0)),
                  pl.BlockSpec((1, tm, ssd_w), lambda i: (1, i, 0)),
                  pl.BlockSpec((1, tm, gdn_w), lambda i: (0, i, 0)),
                  pl.BlockSpec((1, tm, gdn_w), lambda i: (1, i, 0)),
                  pl.BlockSpec((tm, ssd_w + gdn_w), lambda i: (i, 0)),
                  pl.BlockSpec((tm, d), lambda i: (i, 0)),
                  pl.BlockSpec((1, 1, 1, d), lambda i: (i // tiles_per_mod, 0, 0, 2)),
                  pl.BlockSpec((None, ssd_w + gdn_w, d), lambda i: (layer, 0, 0)),
                  pl.BlockSpec((1, ssd_w), lambda i: (0, 0)),
                  pl.BlockSpec((1, gdn_w), lambda i: (0, 0)),
                  pl.BlockSpec((1, d), lambda i: (0, 0))],
        out_specs=pl.BlockSpec((tm, d), lambda i: (i, 0)),
        out_shape=jax.ShapeDtypeStruct((m, d), F32),
        compiler_params=_params("parallel"),
    )(y2, y2, o2, o2, p2d, h2d, gate, w_all, snw, gnw, postw)


def kernel(x, c, ctx, c_ctx, w_ada, b_ada, pre_norm_w, post_norm_w, w_in, conv_ssd_w, conv_ssd_b,
           conv_gdn_w, ssd_a_log, ssd_dt_bias, ssd_d, ssd_norm_w, gdn_a_log, gdn_dt_bias,
           gdn_norm_w, w_out):
    bsz, seq, d = x.shape
    ctx_len = ctx.shape[1]
    depth = w_in.shape[0]
    d_mix = w_out.shape[1]
    ssd_w = d_mix // 2
    gdn_w = d_mix - ssd_w
    ssd_heads = ssd_w // SSD_HEAD_DIM
    gdn_heads = gdn_w // GDN_HEAD_DIM
    bc_w = 2 * SSD_GROUPS * SSD_STATE
    assert ssd_w == gdn_w and ssd_heads + 2 * gdn_heads <= LANE
    assert seq % CHUNK == 0 and ctx_len % CHUNK == 0 and CHUNK % GRID_W == 0

    c_z, c_xs = 0, d_mix
    c_qkv = c_xs + ssd_w
    c_bc = c_qkv + 3 * gdn_w
    c_sm = c_bc + bc_w
    tn = 1024
    wp = -(-(c_sm + N_DIR * LANE) // tn) * tn
    assert c_qkv % gdn_w == 0 and c_xs % ssd_w == 0 and c_bc % bc_w == 0 and c_sm % LANE == 0
    assert c_xs % tn == 0 and c_sm // tn == wp // tn - 1
    beta_lane, g_lane = ssd_heads, ssd_heads + gdn_heads

    o_xbc = d_mix
    o_dt = o_xbc + ssd_w + bc_w
    o_qkv = o_dt + N_DIR * ssd_heads
    o_beta = o_qkv + 3 * gdn_w
    o_alpha = o_beta + N_DIR * gdn_heads

    assert w_in.shape[-1] == o_alpha + N_DIR * gdn_heads
    w_cat = _pack_weights(w_in, wp, d_mix=d_mix, ssd_w=ssd_w, bc_w=bc_w, qkv_w=3 * gdn_w,
                          ssd_heads=ssd_heads, gdn_heads=gdn_heads)

    def conv_layout(ssd_part, gdn_part):
        lead = ssd_part.shape[:-1]
        return jnp.concatenate(
            [jnp.zeros(lead + (d_mix,), F32), ssd_part[..., :ssd_w], gdn_part, ssd_part[..., ssd_w:],
             jnp.zeros(lead + (wp - c_sm,), F32)], axis=-1)

    cw_all = conv_layout(conv_ssd_w.astype(F32), conv_gdn_w.astype(F32))
    cw_all = jnp.concatenate([cw_all, jnp.zeros((depth, 8 - CONV_K, wp), F32)], axis=1)
    cb_all = conv_layout(conv_ssd_b.astype(F32)[:, None], jnp.zeros((depth, 1, 3 * gdn_w), F32))

    def lane_row(vals, lane0):
        return jnp.pad(vals, ((0, 0), (0, 0), (lane0, LANE - lane0 - vals.shape[-1])))

    zeros_rows = jnp.zeros((depth, N_DIR, 6, LANE), F32)
    ssd_par = jnp.concatenate([lane_row(ssd_dt_bias.astype(F32), 0)[:, :, None],
                               lane_row(-jnp.exp(ssd_a_log.astype(F32)), 0)[:, :, None], zeros_rows], axis=2)
    gdn_par = jnp.concatenate([lane_row(gdn_dt_bias.astype(F32), g_lane)[:, :, None],
                               lane_row(-jnp.exp(gdn_a_log.astype(F32)), g_lane)[:, :, None], zeros_rows], axis=2)
    dsk_all = jnp.repeat(ssd_d.astype(F32), SSD_HEAD_DIM, axis=-1)[:, None]
    gnw_all = jnp.tile(gdn_norm_w.astype(F32), (1, gdn_heads))[:, None]
    w_out16 = w_out.astype(BF16)
    head_of_lane = jnp.arange(LANE)[:, None]
    e_q = (jnp.arange(ssd_heads * CHUNK)[None] // CHUNK == head_of_lane).astype(BF16)

    rows = 16
    c_all = jnp.concatenate([c, c_ctx[None], jnp.zeros((rows - bsz - 1, d), c.dtype)], axis=0)
    mod = _modulation(c_all, w_ada, b_ada).reshape(depth, rows, 1, 3 * d)

    h_lat = x.reshape(bsz * seq, d)
    h_ctx = ctx.reshape(bsz * ctx_len, d)
    tm_lat = min(1024, seq)
    tm_ctx = min(1024, bsz * ctx_len)
    tm_out = 256
    conv_cols = (c_xs, c_sm)
    scan_cols = (c_xs, c_qkv, c_bc, c_sm, ssd_w, gdn_w, bc_w)
    ssd_state0 = jnp.zeros((bsz, N_DIR, SSD_GROUPS, SSD_STATE, ssd_w // SSD_GROUPS), F32)
    gdn_state0 = jnp.zeros((bsz, N_DIR, gdn_heads, GDN_HEAD_DIM, GDN_HEAD_DIM), F32)

    for l in range(depth):
        mod_lat = mod[l, :bsz, None]
        mod_ctx = mod[l, bsz:bsz + 1, None]
        pre_w = pre_norm_w[l][None]
        p_lat = _inproj(h_lat, mod_lat, mod_lat, pre_w, w_cat, l, cw_all[l], cb_all[l],
                        rows_per_mod=seq, group=GRID_W, tm=tm_lat, tn=tn, conv_cols=conv_cols)
        p_ctx = _inproj(h_ctx, mod_ctx, mod_ctx, pre_w, w_cat, l, cw_all[l], cb_all[l],
                        rows_per_mod=bsz * ctx_len, group=ctx_len, tm=tm_ctx, tn=tn, conv_cols=conv_cols)
        p_lat3 = p_lat.reshape(bsz, seq, wp)
        p_ctx3 = p_ctx.reshape(bsz, ctx_len, wp)

        y_ctx, ssd_state, o_ctx, gdn_state = _scans(
            p_ctx3, ssd_par[l], gdn_par[l], dsk_all[l], e_q, ssd_state0, gdn_state0,
            cols=scan_cols, beta_lane=beta_lane, g_lane=g_lane)
        y_lat, _, o_lat, _ = _scans(
            p_lat3, ssd_par[l], gdn_par[l], dsk_all[l], e_q, ssd_state, gdn_state,
            cols=scan_cols, beta_lane=beta_lane, g_lane=g_lane)

        snw = ssd_norm_w[l][None].astype(F32)
        post_w = post_norm_w[l][None]
        h_lat_new = _outproj(y_lat, o_lat, p_lat, h_lat, mod_lat, w_out16, l, snw, gnw_all[l], post_w,
                             rows_per_mod=seq, tm=min(tm_out, seq))
        if l < depth - 1:
            h_ctx = _outproj(y_ctx, o_ctx, p_ctx, h_ctx, mod_ctx, w_out16, l, snw, gnw_all[l], post_w,
                             rows_per_mod=bsz * ctx_len, tm=min(tm_out, bsz * ctx_len))
        h_lat = h_lat_new
    return h_lat.reshape(bsz, seq, d)
```

```python
import functools

import jax
import jax.numpy as jnp
from jax import lax
from jax.experimental import pallas as pl
from jax.experimental.pallas import tpu as pltpu

F32 = jnp.float32
BF16 = jnp.bfloat16
HIGHEST = lax.Precision.HIGHEST

EPS = 1e-6
CHUNK = 128
GRID_W = 64
CONV_K = 5
N_DIR = 2
SSD_HEAD_DIM = 64
SSD_STATE = 128
SSD_GROUPS = 2
GDN_HEAD_DIM = 128
LANE = 128
SUBLANE = 8
NORM_UNROLL = 8
SCAN_OUT_DTYPE = jnp.bfloat16
CHUNKS_PER_STEP = 4
GDN_SOLVE_BLOCK = 32
SUB_ROWS = 256
NORM_ROWS = 16
VMEM_LIMIT = 52 * 1024 * 1024

NT = (((1,), (1,)), ((), ()))
TN = (((0,), (0,)), ((), ()))


def _dot(a, b):
    return jnp.dot(a, b, preferred_element_type=F32)


def _dot_nt(a, b):
    return lax.dot_general(a, b, NT, preferred_element_type=F32)


def _dot_tn(a, b):
    return lax.dot_general(a, b, TN, preferred_element_type=F32)


def _sigmoid(x):
    return 1.0 / (1.0 + jnp.exp(-x))


def _silu(x):
    half = 0.5 * x
    return half + half * jnp.tanh(half)


def _softplus(x):
    return jnp.maximum(x, 0.0) + jnp.log(1.0 + jnp.exp(-jnp.abs(x)))


def _params(*sem, flags=None):
    return pltpu.CompilerParams(dimension_semantics=sem, vmem_limit_bytes=VMEM_LIMIT, flags=flags)


def _mod_kernel(c_ref, w_ref, b_ref, o_ref):
    a = _silu(c_ref[...])
    o_ref[0] = jnp.dot(a, w_ref[0], precision=HIGHEST, preferred_element_type=F32) + b_ref[0]


def _modulation(c_all, w_ada, b_ada):
    depth, d, n = w_ada.shape
    rows = c_all.shape[0]
    tn = 768
    return pl.pallas_call(
        _mod_kernel,
        name="adaln_mod",
        grid=(depth, n // tn),
        in_specs=[pl.BlockSpec((rows, d), lambda l, j: (0, 0)),
                  pl.BlockSpec((1, d, tn), lambda l, j: (l, 0, j)),
                  pl.BlockSpec((1, 1, tn), lambda l, j: (l, 0, j))],
        out_specs=pl.BlockSpec((1, rows, tn), lambda l, j: (l, 0, j)),
        out_shape=jax.ShapeDtypeStruct((depth, rows, n), F32),
        compiler_params=_params("parallel", "parallel"),
    )(c_all, w_ada, b_ada.reshape(depth, 1, n))


def _pack_kernel(w_ref, o_ref, *, d_mix, ssd_w, bc_w, qkv_w, ssd_heads, gdn_heads):
    half = d_mix + ssd_w + bc_w
    a = w_ref[0, :, :half]
    b = w_ref[0, :, half:]
    rows = a.shape[0]
    n_dt, n_g = N_DIR * ssd_heads, N_DIR * gdn_heads
    c_qkv = d_mix + ssd_w
    c_bc = c_qkv + qkv_w
    c_sm = c_bc + bc_w
    o_ref[0, :, 0:c_qkv] = a[:, 0:c_qkv].astype(BF16)
    o_ref[0, :, c_qkv:c_bc] = b[:, n_dt:n_dt + qkv_w].astype(BF16)
    o_ref[0, :, c_bc:c_sm] = a[:, c_qkv:c_qkv + bc_w].astype(BF16)
    for dd in range(N_DIR):
        beta0 = n_dt + qkv_w + dd * gdn_heads
        alpha0 = beta0 + n_g
        small = jnp.concatenate(
            [b[:, dd * ssd_heads:(dd + 1) * ssd_heads], b[:, beta0:beta0 + gdn_heads],
             b[:, alpha0:alpha0 + gdn_heads], jnp.zeros((rows, LANE - ssd_heads - 2 * gdn_heads), F32)], axis=1)
        o_ref[0, :, c_sm + dd * LANE:c_sm + (dd + 1) * LANE] = small.astype(BF16)
    pad0 = c_sm + N_DIR * LANE
    o_ref[0, :, pad0:] = jnp.zeros((rows, o_ref.shape[-1] - pad0), BF16)


def _pack_weights(w_in, wp, **dims):
    depth, d, in_dim = w_in.shape
    assert (dims["d_mix"] + dims["ssd_w"] + dims["bc_w"]) % LANE == 0
    rows = 256
    return pl.pallas_call(
        functools.partial(_pack_kernel, **dims),
        name="pack_w_in",
        grid=(depth, d // rows),
        in_specs=[pl.BlockSpec((1, rows, in_dim), lambda l, r: (l, r, 0))],
        out_specs=pl.BlockSpec((1, rows, wp), lambda l, r: (l, r, 0)),
        out_shape=jax.ShapeDtypeStruct((depth, d, wp), BF16),
        compiler_params=_params("parallel", "parallel"),
    )(w_in)


def _inproj_kernel(h_ref, shift_ref, scale_ref, prew_ref, w_ref, cw_ref, cb_ref, o_ref, u_ref,
                   *, group, j_lo, j_hi, mixed_lanes):
    j = pl.program_id(1)
    tm, tn = o_ref.shape
    sub = max(group, min(tm, SUB_ROWS))

    @pl.when(j == 0)
    def _():
        prew = prew_ref[...]
        scale1p = 1.0 + scale_ref[0, 0]
        shift = shift_ref[0, 0]

        def norm_strip(s, carry):
            r0 = pl.multiple_of(s * NORM_ROWS, NORM_ROWS)
            x = h_ref[pl.ds(r0, NORM_ROWS), :]
            y = x * lax.rsqrt(jnp.mean(x * x, axis=-1, keepdims=True) + EPS) * prew
            u_ref[pl.ds(r0, NORM_ROWS), :] = (y * scale1p + shift).astype(BF16)
            return carry

        lax.fori_loop(0, tm // NORM_ROWS, norm_strip, 0, unroll=NORM_UNROLL)

    def conv_tile(lanes):
        cw = cw_ref[:, :lanes]
        bias = cb_ref[:, :lanes]
        row = lax.broadcasted_iota(jnp.int32, (SUBLANE, lanes), 0)
        offs = [k - CONV_K // 2 for k in range(CONV_K)]
        w_mid = [cw[k:k + 1] for k in range(CONV_K)]
        w_first = [jnp.where(row >= -off, w, 0.0) if off < 0 else w for off, w in zip(offs, w_mid)]
        w_last = [jnp.where(row < SUBLANE - off, w, 0.0) if off > 0 else w for off, w in zip(offs, w_mid)]
        last0 = group - SUBLANE

        def taps(pieces, weights, lo, hi):
            out = bias
            for x, w in zip(pieces, weights):
                out = out + x[lo:hi] * w
            return _silu(out)

        n_sub = tm // sub
        acc_next = _dot(u_ref[0:sub, :], w_ref[...])
        for sb in range(n_sub):
            acc = acc_next
            if sb + 1 < n_sub:
                acc_next = _dot(u_ref[(sb + 1) * sub:(sb + 2) * sub, :], w_ref[...])
            if lanes < tn:
                o_ref[sb * sub:(sb + 1) * sub, lanes:] = acc[:, lanes:]
            a = acc[:, :lanes]
            shifted = [a if off == 0 else pltpu.roll(a, (-off) % sub, axis=0) for off in offs]
            for gi in range(sub // group):
                g0 = gi * group
                o0 = sb * sub + g0
                o_ref[o0:o0 + SUBLANE, :lanes] = taps(shifted, w_first, g0, g0 + SUBLANE)
                o_ref[o0 + SUBLANE:o0 + last0, :lanes] = taps(shifted, w_mid, g0 + SUBLANE, g0 + last0)
                o_ref[o0 + last0:o0 + group, :lanes] = taps(shifted, w_last, g0 + last0, g0 + group)

    @pl.when(j < j_lo)
    def _():
        o_ref[...] = _silu(_dot(u_ref[...], w_ref[...]))

    @pl.when((j >= j_lo) & (j < j_hi))
    def _():
        conv_tile(tn)

    @pl.when(j >= j_hi)
    def _():
        conv_tile(mixed_lanes)


def _inproj(h2d, shift, scale, pre_w, w_all, layer, cw, cb, *, rows_per_mod, group, tm, tn, conv_cols):
    m, d = h2d.shape
    wp = w_all.shape[-1]
    tiles_per_mod = rows_per_mod // tm
    kern = functools.partial(_inproj_kernel, group=group, j_lo=conv_cols[0] // tn, j_hi=conv_cols[1] // tn,
                             mixed_lanes=conv_cols[1] % tn)
    return pl.pallas_call(
        kern,
        name="inproj",
        grid=(m // tm, wp // tn),
        in_specs=[pl.BlockSpec((tm, d), lambda i, j: (i, 0)),
                  pl.BlockSpec((1, 1, 1, d), lambda i, j: (i // tiles_per_mod, 0, 0, 0)),
                  pl.BlockSpec((1, 1, 1, d), lambda i, j: (i // tiles_per_mod, 0, 0, 1)),
                  pl.BlockSpec((1, d), lambda i, j: (0, 0)),
                  pl.BlockSpec((None, d, tn), lambda i, j: (layer, 0, j)),
                  pl.BlockSpec((8, tn), lambda i, j: (0, j)),
                  pl.BlockSpec((1, tn), lambda i, j: (0, j))],
        out_specs=pl.BlockSpec((tm, tn), lambda i, j: (i, j)),
        out_shape=jax.ShapeDtypeStruct((m, wp), F32),
        scratch_shapes=[pltpu.VMEM((tm, d), BF16)],
        compiler_params=_params("parallel", "arbitrary"),
    )(h2d, shift, scale, pre_w, w_all, cw, cb)


def _chunk_index(d, p, nc):
    return p + d * (nc - 1 - 2 * p)


def _causal_masks(backward):
    ri = lax.broadcasted_iota(jnp.int32, (CHUNK, CHUNK), 0)
    ci = lax.broadcasted_iota(jnp.int32, (CHUNK, CHUNK), 1)
    return ((ri <= ci), (ri < ci)) if backward else ((ri >= ci), (ri > ci))


def _log_decay_cumsums(sm_ref, spar_ref, gpar_ref, rows, backward):
    tri = _causal_masks(backward)[0].astype(BF16)
    spar = spar_ref[0]
    gpar = gpar_ref[0]
    out = []
    for r in rows:
        sm = sm_ref[0, r:r + CHUNK, :]
        x = _softplus(sm + spar[0:1]) * spar[1:2] + _softplus(sm + gpar[0:1]) * gpar[1:2]
        hi = x.astype(BF16)
        rest = x - hi.astype(F32)
        mid = rest.astype(BF16)
        lo = (rest - mid.astype(F32)).astype(BF16)
        parts = _dot(tri, jnp.concatenate([hi, mid, lo], axis=1))
        out.append(parts[:, :LANE] + parts[:, LANE:2 * LANE] + parts[:, 2 * LANE:])
    return out


def _ssd_chunks(xs_ref, bc_ref, sm_ref, par_ref, dsk_ref, eq_ref, y_ref, hT_ref, cums, backward):
    heads_per_group = xs_ref.shape[-1] // (SSD_GROUPS * SSD_HEAD_DIM)
    gw = heads_per_group * SSD_HEAD_DIM
    incl = _causal_masks(backward)[0]
    last = 0 if backward else CHUNK - 1
    par = par_ref[0]
    cps = xs_ref.shape[1] // CHUNK
    rows = [((cps - 1 - s) if backward else s) * CHUNK for s in range(cps)]
    groups = range(SSD_GROUPS)
    ppg = gw // LANE
    pairs = range(SSD_GROUPS * ppg)
    left = lax.broadcasted_iota(jnp.int32, (CHUNK, LANE), 1) < SSD_HEAD_DIM

    cg, xp, yd, eacum_x, cdec, upd = [], [], [], [], [], []
    for s in range(cps):
        sm = sm_ref[0, rows[s]:rows[s] + CHUNK, :]
        xs = xs_ref[0, rows[s]:rows[s] + CHUNK, :]
        bc = bc_ref[0, rows[s]:rows[s] + CHUNK, :]
        dt = _softplus(sm + par[0:1])
        acum = cums[s]
        dt_t = dt.T
        hi = acum.astype(BF16)
        lo = (acum - hi.astype(F32)).astype(BF16)
        acum_q = _dot(hi, eq_ref[...]) + _dot(lo, eq_ref[...])
        acum_t = (hi.astype(F32) + lo.astype(F32)).T
        bg = [bc[:, g * SSD_STATE:(g + 1) * SSD_STATE].astype(BF16) for g in groups]
        cgs = [bc[:, (SSD_GROUPS + g) * SSD_STATE:(SSD_GROUPS + g + 1) * SSD_STATE].astype(BF16) for g in groups]
        cb = [_dot_nt(cgs[g], bg[g]) for g in groups]

        def intra(hh):
            seg = acum_q[:, hh * CHUNK:(hh + 1) * CHUNK] - acum_t[hh:hh + 1, :]
            return (cb[hh // heads_per_group] * jnp.where(incl, jnp.exp(seg), 0.0)
                    * dt_t[hh:hh + 1, :]).astype(BF16)

        xps = [xs[:, i * LANE:(i + 1) * LANE] for i in pairs]
        m_pair = [jnp.concatenate([intra(2 * i), intra(2 * i + 1)], axis=1) for i in pairs]
        x_bd = [jnp.concatenate([jnp.where(left, x, 0.0), jnp.where(left, 0.0, x)], axis=0).astype(BF16)
                for x in xps]
        yd.append([_dot(m_pair[i], x_bd[i]) for i in pairs])
        acum_x = [jnp.where(left, acum_q[:, 2 * i * CHUNK:2 * i * CHUNK + LANE],
                            acum_q[:, (2 * i + 1) * CHUNK:(2 * i + 1) * CHUNK + LANE]) for i in pairs]
        alast_x = [a[last:last + 1] for a in acum_x]
        dt_x = [jnp.where(left, dt[:, 2 * i:2 * i + 1], dt[:, 2 * i + 1:2 * i + 2]) for i in pairs]
        xw = [(xps[i] * (dt_x[i] * jnp.exp(alast_x[i] - acum_x[i]))).astype(BF16) for i in pairs]
        upd.append([_dot_tn(bg[g], jnp.concatenate(xw[g * ppg:(g + 1) * ppg], axis=1)) for g in groups])
        cdec.append([jnp.exp(jnp.concatenate(alast_x[g * ppg:(g + 1) * ppg], axis=1)) for g in groups])
        eacum_x.append([jnp.exp(a) for a in acum_x])
        cg.append(cgs)
        xp.append(xps)

    h_t = [hT_ref[0, 0, g] for g in groups]
    for s in range(cps):
        yoff = [_dot(cg[s][g], h_t[g].astype(BF16)) for g in groups]
        h_t = [h_t[g] * cdec[s][g] + upd[s][g] for g in groups]
        for i in pairs:
            lo = i * LANE
            yo = yoff[i // ppg][:, (i % ppg) * LANE:(i % ppg + 1) * LANE]
            y = yd[s][i] + yo * eacum_x[s][i]
            if not backward:
                y = y + xp[s][i] * dsk_ref[:, lo:lo + LANE]
            y_ref[0, 0, rows[s]:rows[s] + CHUNK, lo:lo + LANE] = y.astype(y_ref.dtype)
    for g in groups:
        hT_ref[0, 0, g] = h_t[g]


def _diag_block_inverses(a_list, xor_ij, block):
    eye = jnp.where(xor_ij == 0, 1.0, 0.0).astype(BF16)
    in_pair = (xor_ij < 2).astype(BF16)
    t_list = [eye - a * in_pair for a in a_list]
    size = 2
    while size < block:
        couples = ((xor_ij >> (size.bit_length() - 1)) == 1).astype(BF16)
        ta = [_dot(t, a * couples).astype(BF16) for t, a in zip(t_list, a_list)]
        t_list = [t - _dot(x, t).astype(BF16) for t, x in zip(t_list, ta)]
        size *= 2
    return t_list


def _block_substitution(a_list, t_list, rhs_list, block, backward):
    n = a_list[0].shape[0]
    width = rhs_list[0].shape[1]
    order = range(n // block)
    order = list(reversed(order)) if backward else list(order)
    a16, t16 = a_list, t_list

    def place(y, i):
        parts = []
        if i > 0:
            parts.append(jnp.zeros((i * block, width), y.dtype))
        parts.append(y)
        if (i + 1) * block < n:
            parts.append(jnp.zeros((n - (i + 1) * block, width), y.dtype))
        return jnp.concatenate(parts, axis=0) if len(parts) > 1 else y

    x_list = [None] * len(a_list)
    for step, i in enumerate(order):
        rows = slice(i * block, (i + 1) * block)
        y = [r[rows] for r in rhs_list]
        if step > 0:
            y = [yy - _dot(a[rows], x.astype(BF16)) for yy, a, x in zip(y, a16, x_list)]
        xi = [_dot(t[rows], place(yy.astype(BF16), i)) for t, yy in zip(t16, y)]
        x_list = [place(v, i) if x is None else x + place(v, i) for x, v in zip(x_list, xi)]
    return x_list


def _gdn_chunks(q_ref, k_ref, v_ref, sm_ref, o_ref, s_ref, cums, is_backward, *, beta_lane, g_lane):
    heads = range(q_ref.shape[-1] // GDN_HEAD_DIM)
    kk = GDN_HEAD_DIM
    cps = q_ref.shape[1] // CHUNK
    units = [(s, h) for s in range(cps) for h in heads]
    ri = lax.broadcasted_iota(jnp.int32, (CHUNK, CHUNK), 0)
    ci = lax.broadcasted_iota(jnp.int32, (CHUNK, CHUNK), 1)
    xor_ij = ri ^ ci

    def col(x, lane):
        return x[:, lane:lane + 1]

    def scan(backward):
        incl, strict = _causal_masks(backward)
        last = 0 if backward else CHUNK - 1
        rows = [((cps - 1 - s) if backward else s) * CHUNK for s in range(cps)]

        def head_cols(ref, s, h):
            return ref[0, rows[s]:rows[s] + CHUNK, h * kk:(h + 1) * kk]

        eg, eend, cdec, beta_all, gcum, gcum_t = [], [], [], [], [], []
        for s in range(cps):
            sm = sm_ref[0, rows[s]:rows[s] + CHUNK, :]
            beta_all.append(_sigmoid(sm))
            gc = cums[s]
            glast = gc[last:last + 1]
            gcum.append(gc)
            gcum_t.append(gc.T)
            eg.append(jnp.exp(gc))
            eend.append(jnp.exp(glast - gc))
            cdec.append(jnp.exp(glast))
        qn, kn = {}, {}
        for s, h in units:
            q = head_cols(q_ref, s, h)
            k = head_cols(k_ref, s, h)
            qn[s, h] = q * (lax.rsqrt(jnp.sum(q * q, axis=-1, keepdims=True) + EPS) * kk ** -0.5)
            kn[s, h] = k * lax.rsqrt(jnp.sum(k * k, axis=-1, keepdims=True) + EPS)
        beta = {(s, h): col(beta_all[s], beta_lane + h) for s, h in units}
        dec = {(s, h): jnp.where(
            incl, jnp.exp(col(gcum[s], g_lane + h) - gcum_t[s][g_lane + h:g_lane + h + 1, :]), 0.0)
            for s, h in units}
        kb = {x: kn[x] * beta[x] for x in units}
        k16 = {x: kn[x].astype(BF16) for x in units}
        a = [jnp.where(strict, _dot_nt(kb[x].astype(BF16), k16[x]) * dec[x], 0.0).astype(BF16) for x in units]
        attn = {x: (_dot_nt(qn[x].astype(BF16), k16[x]) * dec[x]).astype(BF16) for x in units}
        rhs = [jnp.concatenate([head_cols(v_ref, s, h) * beta[s, h], kb[s, h] * col(eg[s], g_lane + h)], axis=1)
               for s, h in units]
        t = _diag_block_inverses(a, xor_ij, GDN_SOLVE_BLOCK)
        sol = dict(zip(units, _block_substitution(a, t, rhs, GDN_SOLVE_BLOCK, backward)))
        wq = {(s, h): jnp.concatenate([sol[s, h][:, kk:].astype(BF16),
                                       (qn[s, h] * col(eg[s], g_lane + h)).astype(BF16)], axis=0)
              for s, h in units}
        ke = {(s, h): (kn[s, h] * col(eend[s], g_lane + h)).astype(BF16) for s, h in units}

        state = [s_ref[0, 0, h] for h in heads]
        for s in range(cps):
            ws = [_dot(wq[s, h], state[h].astype(BF16)) for h in heads]
            vnew = [(sol[s, h][:, :kk] - ws[h][:CHUNK]).astype(BF16) for h in heads]
            o = [ws[h][CHUNK:] + _dot(attn[s, h], vnew[h]) for h in heads]
            state = [state[h] * col(cdec[s], g_lane + h) + _dot_tn(ke[s, h], vnew[h]) for h in heads]
            for h in heads:
                o_ref[0, 0, rows[s]:rows[s] + CHUNK, h * kk:(h + 1) * kk] = o[h].astype(o_ref.dtype)
        for h in heads:
            s_ref[0, 0, h] = state[h]

    scan(is_backward)


def _scan_kernel(xs_ref, bc_ref, q_ref, k_ref, v_ref, sm_ref, spar_ref, gpar_ref, dsk_ref, eq_ref,
                 h0_ref, s0_ref, y_ref, hT_ref, o_ref, s_ref, *, beta_lane, g_lane):
    d = pl.program_id(1)
    p = pl.program_id(2)

    @pl.when(p == 0)
    def _():
        hT_ref[...] = h0_ref[...]
        s_ref[...] = s0_ref[...]

    def scans(backward):
        cps = sm_ref.shape[1] // CHUNK
        rows = [((cps - 1 - s) if backward else s) * CHUNK for s in range(cps)]
        cums = _log_decay_cumsums(sm_ref, spar_ref, gpar_ref, rows, backward)
        _gdn_chunks(q_ref, k_ref, v_ref, sm_ref, o_ref, s_ref, cums, backward,
                    beta_lane=beta_lane, g_lane=g_lane)
        _ssd_chunks(xs_ref, bc_ref, sm_ref, spar_ref, dsk_ref, eq_ref, y_ref, hT_ref, cums, backward)

    @pl.when(d == 0)
    def _():
        scans(False)

    @pl.when(d == 1)
    def _():
        scans(True)


def _scans(p3d, ssd_par, gdn_par, dsk, e_q, h0, s0, *, cols, beta_lane, g_lane):
    bsz, length, _ = p3d.shape
    nc = length // CHUNK
    c_xs, c_qkv, c_bc, c_sm, w_xs, w_q, w_bc = cols
    cps = min(CHUNKS_PER_STEP, nc)
    nb = nc // cps
    rows = cps * CHUNK
    chunk = functools.partial(_chunk_index, nc=nb)

    def rows_spec(width, col_block):
        return pl.BlockSpec((1, rows, width), lambda b, d, p: (b, chunk(d, p), col_block))

    def state_spec(shape):
        return pl.BlockSpec((1, 1) + shape[2:], lambda b, d, p: (b, d) + (0,) * (len(shape) - 2))

    def out_spec(width):
        return pl.BlockSpec((1, 1, rows, width), lambda b, d, p: (d, b, chunk(d, p), 0))

    return pl.pallas_call(
        functools.partial(_scan_kernel, beta_lane=beta_lane, g_lane=g_lane),
        name="scans",
        grid=(bsz, N_DIR, nb),
        in_specs=[rows_spec(w_xs, c_xs // w_xs), rows_spec(w_bc, c_bc // w_bc)]
        + [rows_spec(w_q, c_qkv // w_q + i) for i in range(3)]
        + [pl.BlockSpec((1, rows, LANE), lambda b, d, p: (b, chunk(d, p), c_sm // LANE + d)),
           pl.BlockSpec((1, 8, LANE), lambda b, d, p: (d, 0, 0)),
           pl.BlockSpec((1, 8, LANE), lambda b, d, p: (d, 0, 0)),
           pl.BlockSpec((1, w_xs), lambda b, d, p: (0, 0)),
           pl.BlockSpec(e_q.shape, lambda b, d, p: (0, 0)),
           state_spec(h0.shape), state_spec(s0.shape)],
        out_specs=[out_spec(w_xs), state_spec(h0.shape), out_spec(w_q), state_spec(s0.shape)],
        out_shape=[jax.ShapeDtypeStruct((N_DIR, bsz, length, w_xs), SCAN_OUT_DTYPE),
                   jax.ShapeDtypeStruct(h0.shape, F32),
                   jax.ShapeDtypeStruct((N_DIR, bsz, length, w_q), SCAN_OUT_DTYPE),
                   jax.ShapeDtypeStruct(s0.shape, F32)],
        compiler_params=_params("parallel", "parallel", "arbitrary"),
    )(p3d, p3d, p3d, p3d, p3d, p3d, ssd_par, gdn_par, dsk, e_q, h0, s0)


def _outproj_kernel(y0_ref, y1_ref, o0_ref, o1_ref, z_ref, h_ref, gate_ref, w_ref,
                    snw_ref, gnw_ref, postw_ref, out_ref):
    zs = z_ref[...]
    ssd_w = y0_ref.shape[-1]
    gw = ssd_w // SSD_GROUPS
    ys = (y0_ref[0].astype(F32) + y1_ref[0].astype(F32)) * zs[:, :ssd_w]
    parts = []
    for g in range(SSD_GROUPS):
        yg = ys[:, g * gw:(g + 1) * gw]
        parts.append(yg * lax.rsqrt(jnp.mean(yg * yg, axis=-1, keepdims=True) + EPS))
    ysn = jnp.concatenate(parts, axis=1) * snw_ref[...]
    og = o0_ref[0].astype(F32) + o1_ref[0].astype(F32)
    parts = []
    for h in range(og.shape[-1] // GDN_HEAD_DIM):
        oh = og[:, h * GDN_HEAD_DIM:(h + 1) * GDN_HEAD_DIM]
        parts.append(oh * lax.rsqrt(jnp.mean(oh * oh, axis=-1, keepdims=True) + EPS))
    ogn = jnp.concatenate(parts, axis=1) * gnw_ref[...] * zs[:, ssd_w:]
    ycat = jnp.concatenate([ysn, ogn], axis=1).astype(BF16)
    m = _dot(ycat, w_ref[...])
    mn = m * lax.rsqrt(jnp.mean(m * m, axis=-1, keepdims=True) + EPS) * postw_ref[...]
    out_ref[...] = h_ref[...] + gate_ref[0, 0] * mn


def _outproj(y, o, p2d, h2d, gate, w_all, layer, snw, gnw, postw, *, rows_per_mod, tm):
    m, d = h2d.shape
    ssd_w = y.shape[-1]
    gdn_w = o.shape[-1]
    tiles_per_mod = rows_per_mod // tm
    y2 = y.reshape(N_DIR, m, ssd_w)
    o2 = o.reshape(N_DIR, m, gdn_w)
    return pl.pallas_call(
        _outproj_kernel,
        name="outproj",
        grid=(m // tm,),
        in_specs=[pl.BlockSpec((1, tm, ssd_w), lambda i: (0, i, 0)),
                  pl.BlockSpec((1, tm, ssd_w), lambda i: (1, i, 0)),
                  pl.BlockSpec((1, tm, gdn_w), lambda i: (0, i, 0)),
                  pl.BlockSpec((1, tm, gdn_w), lambda i: (1, i, 0)),
                  pl.BlockSpec((tm, ssd_w + gdn_w), lambda i: (i, 0)),
                  pl.BlockSpec((tm, d), lambda i: (i, 0)),
                  pl.BlockSpec((1, 1, 1, d), lambda i: (i // tiles_per_mod, 0, 0, 2)),
                  pl.BlockSpec((None, ssd_w + gdn_w, d), lambda i: (layer, 0, 0)),
                  pl.BlockSpec((1, ssd_w), lambda i: (0, 0)),
                  pl.BlockSpec((1, gdn_w), lambda i: (0, 0)),
                  pl.BlockSpec((1, d), lambda i: (0, 0))],
        out_specs=pl.BlockSpec((tm, d), lambda i: (i, 0)),
        out_shape=jax.ShapeDtypeStruct((m, d), F32),
        compiler_params=_params("parallel"),
    )(y2, y2, o2, o2, p2d, h2d, gate, w_all, snw, gnw, postw)


def kernel(x, c, ctx, c_ctx, w_ada, b_ada, pre_norm_w, post_norm_w, w_in, conv_ssd_w, conv_ssd_b,
           conv_gdn_w, ssd_a_log, ssd_dt_bias, ssd_d, ssd_norm_w, gdn_a_log, gdn_dt_bias,
           gdn_norm_w, w_out):
    bsz, seq, d = x.shape
    ctx_len = ctx.shape[1]
    depth = w_in.shape[0]
    d_mix = w_out.shape[1]
    ssd_w = d_mix // 2
    gdn_w = d_mix - ssd_w
    ssd_heads = ssd_w // SSD_HEAD_DIM
    gdn_heads = gdn_w // GDN_HEAD_DIM
    bc_w = 2 * SSD_GROUPS * SSD_STATE
    assert ssd_w == gdn_w and ssd_heads + 2 * gdn_heads <= LANE
    assert seq % CHUNK == 0 and ctx_len % CHUNK == 0 and CHUNK % GRID_W == 0

    c_z, c_xs = 0, d_mix
    c_qkv = c_xs + ssd_w
    c_bc = c_qkv + 3 * gdn_w
    c_sm = c_bc + bc_w
    tn = 1024
    wp = -(-(c_sm + N_DIR * LANE) // tn) * tn
    assert c_qkv % gdn_w == 0 and c_xs % ssd_w == 0 and c_bc % bc_w == 0 and c_sm % LANE == 0
    assert c_xs % tn == 0 and c_sm // tn == wp // tn - 1
    beta_lane, g_lane = ssd_heads, ssd_heads + gdn_heads

    o_xbc = d_mix
    o_dt = o_xbc + ssd_w + bc_w
    o_qkv = o_dt + N_DIR * ssd_heads
    o_beta = o_qkv + 3 * gdn_w
    o_alpha = o_beta + N_DIR * gdn_heads

    assert w_in.shape[-1] == o_alpha + N_DIR * gdn_heads
    w_cat = _pack_weights(w_in, wp, d_mix=d_mix, ssd_w=ssd_w, bc_w=bc_w, qkv_w=3 * gdn_w,
                          ssd_heads=ssd_heads, gdn_heads=gdn_heads)

    def conv_layout(ssd_part, gdn_part):
        lead = ssd_part.shape[:-1]
        return jnp.concatenate(
            [jnp.zeros(lead + (d_mix,), F32), ssd_part[..., :ssd_w], gdn_part, ssd_part[..., ssd_w:],
             jnp.zeros(lead + (wp - c_sm,), F32)], axis=-1)

    cw_all = conv_layout(conv_ssd_w.astype(F32), conv_gdn_w.astype(F32))
    cw_all = jnp.concatenate([cw_all, jnp.zeros((depth, 8 - CONV_K, wp), F32)], axis=1)
    cb_all = conv_layout(conv_ssd_b.astype(F32)[:, None], jnp.zeros((depth, 1, 3 * gdn_w), F32))

    def lane_row(vals, lane0):
        return jnp.pad(vals, ((0, 0), (0, 0), (lane0, LANE - lane0 - vals.shape[-1])))

    zeros_rows = jnp.zeros((depth, N_DIR, 6, LANE), F32)
    ssd_par = jnp.concatenate([lane_row(ssd_dt_bias.astype(F32), 0)[:, :, None],
                               lane_row(-jnp.exp(ssd_a_log.astype(F32)), 0)[:, :, None], zeros_rows], axis=2)
    gdn_par = jnp.concatenate([lane_row(gdn_dt_bias.astype(F32), g_lane)[:, :, None],
                               lane_row(-jnp.exp(gdn_a_log.astype(F32)), g_lane)[:, :, None], zeros_rows], axis=2)
    dsk_all = jnp.repeat(ssd_d.astype(F32), SSD_HEAD_DIM, axis=-1)[:, None]
    gnw_all = jnp.tile(gdn_norm_w.astype(F32), (1, gdn_heads))[:, None]
    w_out16 = w_out.astype(BF16)
    head_of_lane = jnp.arange(LANE)[:, None]
    e_q = (jnp.arange(ssd_heads * CHUNK)[None] // CHUNK == head_of_lane).astype(BF16)

    rows = 16
    c_all = jnp.concatenate([c, c_ctx[None], jnp.zeros((rows - bsz - 1, d), c.dtype)], axis=0)
    mod = _modulation(c_all, w_ada, b_ada).reshape(depth, rows, 1, 3 * d)

    h_lat = x.reshape(bsz * seq, d)
    h_ctx = ctx.reshape(bsz * ctx_len, d)
    tm_lat = min(1024, seq)
    tm_ctx = min(1024, bsz * ctx_len)
    tm_out = 256
    conv_cols = (c_xs, c_sm)
    scan_cols = (c_xs, c_qkv, c_bc, c_sm, ssd_w, gdn_w, bc_w)
    ssd_state0 = jnp.zeros((bsz, N_DIR, SSD_GROUPS, SSD_STATE, ssd_w // SSD_GROUPS), F32)
    gdn_state0 = jnp.zeros((bsz, N_DIR, gdn_heads, GDN_HEAD_DIM, GDN_HEAD_DIM), F32)

    for l in range(depth):
        mod_lat = mod[l, :bsz, None]
        mod_ctx = mod[l, bsz:bsz + 1, None]
        pre_w = pre_norm_w[l][None]
        p_lat = _inproj(h_lat, mod_lat, mod_lat, pre_w, w_cat, l, cw_all[l], cb_all[l],
                        rows_per_mod=seq, group=GRID_W, tm=tm_lat, tn=tn, conv_cols=conv_cols)
        p_ctx = _inproj(h_ctx, mod_ctx, mod_ctx, pre_w, w_cat, l, cw_all[l], cb_all[l],
                        rows_per_mod=bsz * ctx_len, group=ctx_len, tm=tm_ctx, tn=tn, conv_cols=conv_cols)
        p_lat3 = p_lat.reshape(bsz, seq, wp)
        p_ctx3 = p_ctx.reshape(bsz, ctx_len, wp)

        y_ctx, ssd_state, o_ctx, gdn_state = _scans(
            p_ctx3, ssd_par[l], gdn_par[l], dsk_all[l], e_q, ssd_state0, gdn_state0,
            cols=scan_cols, beta_lane=beta_lane, g_lane=g_lane)
        y_lat, _, o_lat, _ = _scans(
            p_lat3, ssd_par[l], gdn_par[l], dsk_all[l], e_q, ssd_state, gdn_state,
            cols=scan_cols, beta_lane=beta_lane, g_lane=g_lane)

        snw = ssd_norm_w[l][None].astype(F32)
        post_w = post_norm_w[l][None]
        h_lat_new = _outproj(y_lat, o_lat, p_lat, h_lat, mod_lat, w_out16, l, snw, gnw_all[l], post_w,
                             rows_per_mod=seq, tm=min(tm_out, seq))
        if l < depth - 1:
            h_ctx = _outproj(y_ctx, o_ctx, p_ctx, h_ctx, mod_ctx, w_out16, l, snw, gnw_all[l], post_w,
                             rows_per_mod=bsz * ctx_len, tm=min(tm_out, bsz * ctx_len))
        h_lat = h_lat_new
    return h_lat.reshape(bsz, seq, d)
```

```python
import functools

import jax
import jax.numpy as jnp
from jax import lax
from jax.experimental import pallas as pl
from jax.experimental.pallas import tpu as pltpu

F32 = jnp.float32
BF16 = jnp.bfloat16
HIGHEST = lax.Precision.HIGHEST

EPS = 1e-6
CHUNK = 128
GRID_W = 64
CONV_K = 5
N_DIR = 2
SSD_HEAD_DIM = 64
SSD_STATE = 128
SSD_GROUPS = 2
GDN_HEAD_DIM = 128
LANE = 128
SUBLANE = 8
NORM_UNROLL = 8
SCAN_OUT_DTYPE = jnp.bfloat16
CHUNKS_PER_STEP = 4
GDN_SOLVE_BLOCK = 32
SUB_ROWS = 512
NORM_ROWS = 16
VMEM_LIMIT = 52 * 1024 * 1024

NT = (((1,), (1,)), ((), ()))
TN = (((0,), (0,)), ((), ()))


def _dot(a, b):
    return jnp.dot(a, b, preferred_element_type=F32)


def _dot_nt(a, b):
    return lax.dot_general(a, b, NT, preferred_element_type=F32)


def _dot_tn(a, b):
    return lax.dot_general(a, b, TN, preferred_element_type=F32)


def _sigmoid(x):
    return 1.0 / (1.0 + jnp.exp(-x))


def _silu(x):
    half = 0.5 * x
    return half + half * jnp.tanh(half)


def _softplus(x):
    return jnp.maximum(x, 0.0) + jnp.log(1.0 + jnp.exp(-jnp.abs(x)))


def _params(*sem, flags=None):
    return pltpu.CompilerParams(dimension_semantics=sem, vmem_limit_bytes=VMEM_LIMIT, flags=flags)


def _mod_kernel(c_ref, w_ref, b_ref, o_ref):
    a = _silu(c_ref[...])
    o_ref[0] = jnp.dot(a, w_ref[0], precision=HIGHEST, preferred_element_type=F32) + b_ref[0]


def _modulation(c_all, w_ada, b_ada):
    depth, d, n = w_ada.shape
    rows = c_all.shape[0]
    tn = 768
    return pl.pallas_call(
        _mod_kernel,
        name="adaln_mod",
        grid=(depth, n // tn),
        in_specs=[pl.BlockSpec((rows, d), lambda l, j: (0, 0)),
                  pl.BlockSpec((1, d, tn), lambda l, j: (l, 0, j)),
                  pl.BlockSpec((1, 1, tn), lambda l, j: (l, 0, j))],
        out_specs=pl.BlockSpec((1, rows, tn), lambda l, j: (l, 0, j)),
        out_shape=jax.ShapeDtypeStruct((depth, rows, n), F32),
        compiler_params=_params("parallel", "parallel"),
    )(c_all, w_ada, b_ada.reshape(depth, 1, n))


def _pack_kernel(w_ref, o_ref, *, d_mix, ssd_w, bc_w, qkv_w, ssd_heads, gdn_heads):
    half = d_mix + ssd_w + bc_w
    a = w_ref[0, :, :half]
    b = w_ref[0, :, half:]
    rows = a.shape[0]
    n_dt, n_g = N_DIR * ssd_heads, N_DIR * gdn_heads
    c_qkv = d_mix + ssd_w
    c_bc = c_qkv + qkv_w
    c_sm = c_bc + bc_w
    o_ref[0, :, 0:c_qkv] = a[:, 0:c_qkv].astype(BF16)
    o_ref[0, :, c_qkv:c_bc] = b[:, n_dt:n_dt + qkv_w].astype(BF16)
    o_ref[0, :, c_bc:c_sm] = a[:, c_qkv:c_qkv + bc_w].astype(BF16)
    for dd in range(N_DIR):
        beta0 = n_dt + qkv_w + dd * gdn_heads
        alpha0 = beta0 + n_g
        small = jnp.concatenate(
            [b[:, dd * ssd_heads:(dd + 1) * ssd_heads], b[:, beta0:beta0 + gdn_heads],
             b[:, alpha0:alpha0 + gdn_heads], jnp.zeros((rows, LANE - ssd_heads - 2 * gdn_heads), F32)], axis=1)
        o_ref[0, :, c_sm + dd * LANE:c_sm + (dd + 1) * LANE] = small.astype(BF16)
    pad0 = c_sm + N_DIR * LANE
    o_ref[0, :, pad0:] = jnp.zeros((rows, o_ref.shape[-1] - pad0), BF16)


def _pack_weights(w_in, wp, **dims):
    depth, d, in_dim = w_in.shape
    assert (dims["d_mix"] + dims["ssd_w"] + dims["bc_w"]) % LANE == 0
    rows = 256
    return pl.pallas_call(
        functools.partial(_pack_kernel, **dims),
        name="pack_w_in",
        grid=(depth, d // rows),
        in_specs=[pl.BlockSpec((1, rows, in_dim), lambda l, r: (l, r, 0))],
        out_specs=pl.BlockSpec((1, rows, wp), lambda l, r: (l, r, 0)),
        out_shape=jax.ShapeDtypeStruct((depth, d, wp), BF16),
        compiler_params=_params("parallel", "parallel"),
    )(w_in)


def _inproj_kernel(h_ref, shift_ref, scale_ref, prew_ref, w_ref, cw_ref, cb_ref, o_ref, u_ref,
                   *, group, j_lo, j_hi, mixed_lanes):
    j = pl.program_id(1)
    tm, tn = o_ref.shape
    sub = max(group, min(tm, SUB_ROWS))

    @pl.when(j == 0)
    def _():
        gain = prew_ref[...] * (1.0 + scale_ref[0, 0])
        shift = shift_ref[0, 0]

        def norm_strip(s, carry):
            r0 = pl.multiple_of(s * NORM_ROWS, NORM_ROWS)
            x = h_ref[pl.ds(r0, NORM_ROWS), :]
            y = x * lax.rsqrt(jnp.mean(x * x, axis=-1, keepdims=True) + EPS)
            u_ref[pl.ds(r0, NORM_ROWS), :] = (y * gain + shift).astype(BF16)
            return carry

        lax.fori_loop(0, tm // NORM_ROWS, norm_strip, 0, unroll=NORM_UNROLL)

    def conv_tile(lanes):
        cw = cw_ref[:, :lanes]
        bias = cb_ref[:, :lanes]
        row = lax.broadcasted_iota(jnp.int32, (SUBLANE, lanes), 0)
        offs = [k - CONV_K // 2 for k in range(CONV_K)]
        w_mid = [cw[k:k + 1] for k in range(CONV_K)]
        w_first = [jnp.where(row >= -off, w, 0.0) if off < 0 else w for off, w in zip(offs, w_mid)]
        w_last = [jnp.where(row < SUBLANE - off, w, 0.0) if off > 0 else w for off, w in zip(offs, w_mid)]
        last0 = group - SUBLANE

        def taps(pieces, weights, lo, hi):
            out = bias
            for x, w in zip(pieces, weights):
                out = out + x[lo:hi] * w
            return _silu(out)

        n_sub = tm // sub
        acc_next = _dot(u_ref[0:sub, :], w_ref[...])
        for sb in range(n_sub):
            acc = acc_next
            if sb + 1 < n_sub:
                acc_next = _dot(u_ref[(sb + 1) * sub:(sb + 2) * sub, :], w_ref[...])
            if lanes < tn:
                o_ref[sb * sub:(sb + 1) * sub, lanes:] = acc[:, lanes:]
            a = acc[:, :lanes]
            shifted = [a if off == 0 else pltpu.roll(a, (-off) % sub, axis=0) for off in offs]
            for gi in range(sub // group):
                g0 = gi * group
                o0 = sb * sub + g0
                o_ref[o0:o0 + SUBLANE, :lanes] = taps(shifted, w_first, g0, g0 + SUBLANE)
                o_ref[o0 + SUBLANE:o0 + last0, :lanes] = taps(shifted, w_mid, g0 + SUBLANE, g0 + last0)
                o_ref[o0 + last0:o0 + group, :lanes] = taps(shifted, w_last, g0 + last0, g0 + group)

    @pl.when(j < j_lo)
    def _():
        o_ref[...] = _silu(_dot(u_ref[...], w_ref[...]))

    @pl.when((j >= j_lo) & (j < j_hi))
    def _():
        conv_tile(tn)

    @pl.when(j >= j_hi)
    def _():
        conv_tile(mixed_lanes)


def _inproj(h2d, shift, scale, pre_w, w_all, layer, cw, cb, *, rows_per_mod, group, tm, tn, conv_cols):
    m, d = h2d.shape
    wp = w_all.shape[-1]
    tiles_per_mod = rows_per_mod // tm
    kern = functools.partial(_inproj_kernel, group=group, j_lo=conv_cols[0] // tn, j_hi=conv_cols[1] // tn,
                             mixed_lanes=conv_cols[1] % tn)
    return pl.pallas_call(
        kern,
        name="inproj",
        grid=(m // tm, wp // tn),
        in_specs=[pl.BlockSpec((tm, d), lambda i, j: (i, 0)),
                  pl.BlockSpec((1, 1, 1, d), lambda i, j: (i // tiles_per_mod, 0, 0, 0)),
                  pl.BlockSpec((1, 1, 1, d), lambda i, j: (i // tiles_per_mod, 0, 0, 1)),
                  pl.BlockSpec((1, d), lambda i, j: (0, 0)),
                  pl.BlockSpec((None, d, tn), lambda i, j: (layer, 0, j)),
                  pl.BlockSpec((8, tn), lambda i, j: (0, j)),
                  pl.BlockSpec((1, tn), lambda i, j: (0, j))],
        out_specs=pl.BlockSpec((tm, tn), lambda i, j: (i, j)),
        out_shape=jax.ShapeDtypeStruct((m, wp), F32),
        scratch_shapes=[pltpu.VMEM((tm, d), BF16)],
        compiler_params=_params("parallel", "arbitrary"),
    )(h2d, shift, scale, pre_w, w_all, cw, cb)


def _chunk_index(d, p, nc):
    return p + d * (nc - 1 - 2 * p)


def _causal_masks(backward):
    ri = lax.broadcasted_iota(jnp.int32, (CHUNK, CHUNK), 0)
    ci = lax.broadcasted_iota(jnp.int32, (CHUNK, CHUNK), 1)
    return ((ri <= ci), (ri < ci)) if backward else ((ri >= ci), (ri > ci))


def _log_decay_cumsums(sm_ref, spar_ref, gpar_ref, rows, backward):
    tri = _causal_masks(backward)[0].astype(BF16)
    spar = spar_ref[0]
    gpar = gpar_ref[0]
    out = []
    for r in rows:
        sm = sm_ref[0, r:r + CHUNK, :]
        x = _softplus(sm + spar[0:1]) * spar[1:2] + _softplus(sm + gpar[0:1]) * gpar[1:2]
        hi = x.astype(BF16)
        rest = x - hi.astype(F32)
        mid = rest.astype(BF16)
        lo = (rest - mid.astype(F32)).astype(BF16)
        parts = _dot(tri, jnp.concatenate([hi, mid, lo], axis=1))
        out.append(parts[:, :LANE] + parts[:, LANE:2 * LANE] + parts[:, 2 * LANE:])
    return out


def _ssd_chunks(xs_ref, bc_ref, sm_ref, par_ref, dsk_ref, eq_ref, y_ref, hT_ref, cums, backward):
    heads_per_group = xs_ref.shape[-1] // (SSD_GROUPS * SSD_HEAD_DIM)
    gw = heads_per_group * SSD_HEAD_DIM
    incl = _causal_masks(backward)[0]
    last = 0 if backward else CHUNK - 1
    par = par_ref[0]
    cps = xs_ref.shape[1] // CHUNK
    rows = [((cps - 1 - s) if backward else s) * CHUNK for s in range(cps)]
    groups = range(SSD_GROUPS)
    ppg = gw // LANE
    pairs = range(SSD_GROUPS * ppg)
    left = lax.broadcasted_iota(jnp.int32, (CHUNK, LANE), 1) < SSD_HEAD_DIM

    cg, xp, yd, eacum_x, cdec, upd = [], [], [], [], [], []
    for s in range(cps):
        sm = sm_ref[0, rows[s]:rows[s] + CHUNK, :]
        xs = xs_ref[0, rows[s]:rows[s] + CHUNK, :]
        bc = bc_ref[0, rows[s]:rows[s] + CHUNK, :]
        dt = _softplus(sm + par[0:1])
        acum = cums[s]
        dt_t = dt.T
        hi = acum.astype(BF16)
        lo = (acum - hi.astype(F32)).astype(BF16)
        acum_q = _dot(hi, eq_ref[...]) + _dot(lo, eq_ref[...])
        acum_t = (hi.astype(F32) + lo.astype(F32)).T
        bg = [bc[:, g * SSD_STATE:(g + 1) * SSD_STATE].astype(BF16) for g in groups]
        cgs = [bc[:, (SSD_GROUPS + g) * SSD_STATE:(SSD_GROUPS + g + 1) * SSD_STATE].astype(BF16) for g in groups]
        cb = [_dot_nt(cgs[g], bg[g]) for g in groups]

        def intra(hh):
            seg = acum_q[:, hh * CHUNK:(hh + 1) * CHUNK] - acum_t[hh:hh + 1, :]
            return (cb[hh // heads_per_group] * jnp.where(incl, jnp.exp(seg), 0.0)
                    * dt_t[hh:hh + 1, :]).astype(BF16)

        xps = [xs[:, i * LANE:(i + 1) * LANE] for i in pairs]
        m_pair = [jnp.concatenate([intra(2 * i), intra(2 * i + 1)], axis=1) for i in pairs]
        x_bd = [jnp.concatenate([jnp.where(left, x, 0.0), jnp.where(left, 0.0, x)], axis=0).astype(BF16)
                for x in xps]
        yd.append([_dot(m_pair[i], x_bd[i]) for i in pairs])
        acum_x = [jnp.where(left, acum_q[:, 2 * i * CHUNK:2 * i * CHUNK + LANE],
                            acum_q[:, (2 * i + 1) * CHUNK:(2 * i + 1) * CHUNK + LANE]) for i in pairs]
        alast_x = [a[last:last + 1] for a in acum_x]
        dt_x = [jnp.where(left, dt[:, 2 * i:2 * i + 1], dt[:, 2 * i + 1:2 * i + 2]) for i in pairs]
        xw = [(xps[i] * (dt_x[i] * jnp.exp(alast_x[i] - acum_x[i]))).astype(BF16) for i in pairs]
        upd.append([_dot_tn(bg[g], jnp.concatenate(xw[g * ppg:(g + 1) * ppg], axis=1)) for g in groups])
        cdec.append([jnp.exp(jnp.concatenate(alast_x[g * ppg:(g + 1) * ppg], axis=1)) for g in groups])
        eacum_x.append([jnp.exp(a) for a in acum_x])
        cg.append(cgs)
        xp.append(xps)

    h_t = [hT_ref[0, 0, g] for g in groups]
    for s in range(cps):
        yoff = [_dot(cg[s][g], h_t[g].astype(BF16)) for g in groups]
        h_t = [h_t[g] * cdec[s][g] + upd[s][g] for g in groups]
        for i in pairs:
            lo = i * LANE
            yo = yoff[i // ppg][:, (i % ppg) * LANE:(i % ppg + 1) * LANE]
            y = yd[s][i] + yo * eacum_x[s][i]
            if not backward:
                y = y + xp[s][i] * dsk_ref[:, lo:lo + LANE]
            y_ref[0, 0, rows[s]:rows[s] + CHUNK, lo:lo + LANE] = y.astype(y_ref.dtype)
    for g in groups:
        hT_ref[0, 0, g] = h_t[g]


def _diag_block_inverses(a_list, xor_ij, block):
    eye = jnp.where(xor_ij == 0, 1.0, 0.0).astype(BF16)
    in_pair = (xor_ij < 2).astype(BF16)
    t_list = [eye - a * in_pair for a in a_list]
    size = 2
    while size < block:
        couples = ((xor_ij >> (size.bit_length() - 1)) == 1).astype(BF16)
        ta = [_dot(t, a * couples).astype(BF16) for t, a in zip(t_list, a_list)]
        t_list = [t - _dot(x, t).astype(BF16) for t, x in zip(t_list, ta)]
        size *= 2
    return t_list


def _block_substitution(a_list, t_list, rhs_list, block, backward):
    n = a_list[0].shape[0]
    width = rhs_list[0].shape[1]
    order = range(n // block)
    order = list(reversed(order)) if backward else list(order)
    a16, t16 = a_list, t_list

    def place(y, i):
        parts = []
        if i > 0:
            parts.append(jnp.zeros((i * block, width), y.dtype))
        parts.append(y)
        if (i + 1) * block < n:
            parts.append(jnp.zeros((n - (i + 1) * block, width), y.dtype))
        return jnp.concatenate(parts, axis=0) if len(parts) > 1 else y

    x_list = [None] * len(a_list)
    for step, i in enumerate(order):
        rows = slice(i * block, (i + 1) * block)
        y = [r[rows] for r in rhs_list]
        if step > 0:
            y = [yy - _dot(a[rows], x.astype(BF16)) for yy, a, x in zip(y, a16, x_list)]
        xi = [_dot(t[rows], place(yy.astype(BF16), i)) for t, yy in zip(t16, y)]
        x_list = [place(v, i) if x is None else x + place(v, i) for x, v in zip(x_list, xi)]
    return x_list


def _gdn_chunks(q_ref, k_ref, v_ref, sm_ref, o_ref, s_ref, cums, is_backward, *, beta_lane, g_lane):
    heads = range(q_ref.shape[-1] // GDN_HEAD_DIM)
    kk = GDN_HEAD_DIM
    cps = q_ref.shape[1] // CHUNK
    units = [(s, h) for s in range(cps) for h in heads]
    ri = lax.broadcasted_iota(jnp.int32, (CHUNK, CHUNK), 0)
    ci = lax.broadcasted_iota(jnp.int32, (CHUNK, CHUNK), 1)
    xor_ij = ri ^ ci

    def col(x, lane):
        return x[:, lane:lane + 1]

    def scan(backward):
        incl, strict = _causal_masks(backward)
        last = 0 if backward else CHUNK - 1
        rows = [((cps - 1 - s) if backward else s) * CHUNK for s in range(cps)]

        def head_cols(ref, s, h):
            return ref[0, rows[s]:rows[s] + CHUNK, h * kk:(h + 1) * kk]

        eg, eend, cdec, beta_all, gcum, gcum_t = [], [], [], [], [], []
        for s in range(cps):
            sm = sm_ref[0, rows[s]:rows[s] + CHUNK, :]
            beta_all.append(_sigmoid(sm))
            gc = cums[s]
            glast = gc[last:last + 1]
            gcum.append(gc)
            gcum_t.append(gc.T)
            eg.append(jnp.exp(gc))
            eend.append(jnp.exp(glast - gc))
            cdec.append(jnp.exp(glast))
        qn, kn = {}, {}
        for s, h in units:
            q = head_cols(q_ref, s, h)
            k = head_cols(k_ref, s, h)
            qn[s, h] = q * (lax.rsqrt(jnp.sum(q * q, axis=-1, keepdims=True) + EPS) * kk ** -0.5)
            kn[s, h] = k * lax.rsqrt(jnp.sum(k * k, axis=-1, keepdims=True) + EPS)
        beta = {(s, h): col(beta_all[s], beta_lane + h) for s, h in units}
        dec = {(s, h): jnp.where(
            incl, jnp.exp(col(gcum[s], g_lane + h) - gcum_t[s][g_lane + h:g_lane + h + 1, :]), 0.0)
            for s, h in units}
        kb = {x: kn[x] * beta[x] for x in units}
        k16 = {x: kn[x].astype(BF16) for x in units}
        a = [jnp.where(strict, _dot_nt(kb[x].astype(BF16), k16[x]) * dec[x], 0.0).astype(BF16) for x in units]
        attn = {x: (_dot_nt(qn[x].astype(BF16), k16[x]) * dec[x]).astype(BF16) for x in units}
        rhs = [jnp.concatenate([head_cols(v_ref, s, h) * beta[s, h], kb[s, h] * col(eg[s], g_lane + h)], axis=1)
               for s, h in units]
        t = _diag_block_inverses(a, xor_ij, GDN_SOLVE_BLOCK)
        sol = dict(zip(units, _block_substitution(a, t, rhs, GDN_SOLVE_BLOCK, backward)))
        wq = {(s, h): jnp.concatenate([sol[s, h][:, kk:].astype(BF16),
                                       (qn[s, h] * col(eg[s], g_lane + h)).astype(BF16)], axis=0)
              for s, h in units}
        ke = {(s, h): (kn[s, h] * col(eend[s], g_lane + h)).astype(BF16) for s, h in units}

        state = [s_ref[0, 0, h] for h in heads]
        for s in range(cps):
            ws = [_dot(wq[s, h], state[h].astype(BF16)) for h in heads]
            vnew = [(sol[s, h][:, :kk] - ws[h][:CHUNK]).astype(BF16) for h in heads]
            o = [ws[h][CHUNK:] + _dot(attn[s, h], vnew[h]) for h in heads]
            state = [state[h] * col(cdec[s], g_lane + h) + _dot_tn(ke[s, h], vnew[h]) for h in heads]
            for h in heads:
                o_ref[0, 0, rows[s]:rows[s] + CHUNK, h * kk:(h + 1) * kk] = o[h].astype(o_ref.dtype)
        for h in heads:
            s_ref[0, 0, h] = state[h]

    scan(is_backward)


def _scan_kernel(xs_ref, bc_ref, q_ref, k_ref, v_ref, sm_ref, spar_ref, gpar_ref, dsk_ref, eq_ref,
                 h0_ref, s0_ref, y_ref, hT_ref, o_ref, s_ref, *, beta_lane, g_lane):
    d = pl.program_id(1)
    p = pl.program_id(2)

    @pl.when(p == 0)
    def _():
        hT_ref[...] = h0_ref[...]
        s_ref[...] = s0_ref[...]

    def scans(backward):
        cps = sm_ref.shape[1] // CHUNK
        rows = [((cps - 1 - s) if backward else s) * CHUNK for s in range(cps)]
        cums = _log_decay_cumsums(sm_ref, spar_ref, gpar_ref, rows, backward)
        _gdn_chunks(q_ref, k_ref, v_ref, sm_ref, o_ref, s_ref, cums, backward,
                    beta_lane=beta_lane, g_lane=g_lane)
        _ssd_chunks(xs_ref, bc_ref, sm_ref, spar_ref, dsk_ref, eq_ref, y_ref, hT_ref, cums, backward)

    @pl.when(d == 0)
    def _():
        scans(False)

    @pl.when(d == 1)
    def _():
        scans(True)


def _scans(p3d, ssd_par, gdn_par, dsk, e_q, h0, s0, *, cols, beta_lane, g_lane):
    bsz, length, _ = p3d.shape
    nc = length // CHUNK
    c_xs, c_qkv, c_bc, c_sm, w_xs, w_q, w_bc = cols
    cps = min(CHUNKS_PER_STEP, nc)
    nb = nc // cps
    rows = cps * CHUNK
    chunk = functools.partial(_chunk_index, nc=nb)

    def rows_spec(width, col_block):
        return pl.BlockSpec((1, rows, width), lambda b, d, p: (b, chunk(d, p), col_block))

    def state_spec(shape):
        return pl.BlockSpec((1, 1) + shape[2:], lambda b, d, p: (b, d) + (0,) * (len(shape) - 2))

    def out_spec(width):
        return pl.BlockSpec((1, 1, rows, width), lambda b, d, p: (d, b, chunk(d, p), 0))

    return pl.pallas_call(
        functools.partial(_scan_kernel, beta_lane=beta_lane, g_lane=g_lane),
        name="scans",
        grid=(bsz, N_DIR, nb),
        in_specs=[rows_spec(w_xs, c_xs // w_xs), rows_spec(w_bc, c_bc // w_bc)]
        + [rows_spec(w_q, c_qkv // w_q + i) for i in range(3)]
        + [pl.BlockSpec((1, rows, LANE), lambda b, d, p: (b, chunk(d, p), c_sm // LANE + d)),
           pl.BlockSpec((1, 8, LANE), lambda b, d, p: (d, 0, 0)),
           pl.BlockSpec((1, 8, LANE), lambda b, d, p: (d, 0, 0)),
           pl.BlockSpec((1, w_xs), lambda b, d, p: (0, 0)),
           pl.BlockSpec(e_q.shape, lambda b, d, p: (0, 0)),
           state_spec(h0.shape), state_spec(s0.shape)],
        out_specs=[out_spec(w_xs), state_spec(h0.shape), out_spec(w_q), state_spec(s0.shape)],
        out_shape=[jax.ShapeDtypeStruct((N_DIR, bsz, length, w_xs), SCAN_OUT_DTYPE),
                   jax.ShapeDtypeStruct(h0.shape, F32),
                   jax.ShapeDtypeStruct((N_DIR, bsz, length, w_q), SCAN_OUT_DTYPE),
                   jax.ShapeDtypeStruct(s0.shape, F32)],
        compiler_params=_params("parallel", "parallel", "arbitrary"),
    )(p3d, p3d, p3d, p3d, p3d, p3d, ssd_par, gdn_par, dsk, e_q, h0, s0)


def _outproj_kernel(y0_ref, y1_ref, o0_ref, o1_ref, z_ref, h_ref, gate_ref, w_ref,
                    snw_ref, gnw_ref, postw_ref, out_ref):
    zs = z_ref[...]
    ssd_w = y0_ref.shape[-1]
    gw = ssd_w // SSD_GROUPS
    ys = (y0_ref[0].astype(F32) + y1_ref[0].astype(F32)) * zs[:, :ssd_w]
    parts = []
    for g in range(SSD_GROUPS):
        yg = ys[:, g * gw:(g + 1) * gw]
        parts.append(yg * lax.rsqrt(jnp.mean(yg * yg, axis=-1, keepdims=True) + EPS))
    ysn = jnp.concatenate(parts, axis=1) * snw_ref[...]
    og = o0_ref[0].astype(F32) + o1_ref[0].astype(F32)
    parts = []
    for h in range(og.shape[-1] // GDN_HEAD_DIM):
        oh = og[:, h * GDN_HEAD_DIM:(h + 1) * GDN_HEAD_DIM]
        parts.append(oh * lax.rsqrt(jnp.mean(oh * oh, axis=-1, keepdims=True) + EPS))
    ogn = jnp.concatenate(parts, axis=1) * gnw_ref[...] * zs[:, ssd_w:]
    ycat = jnp.concatenate([ysn, ogn], axis=1).astype(BF16)
    m = _dot(ycat, w_ref[...])
    mn = m * lax.rsqrt(jnp.mean(m * m, axis=-1, keepdims=True) + EPS) * postw_ref[...]
    out_ref[...] = h_ref[...] + gate_ref[0, 0] * mn


def _outproj(y, o, p2d, h2d, gate, w_all, layer, snw, gnw, postw, *, rows_per_mod, tm):
    m, d = h2d.shape
    ssd_w = y.shape[-1]
    gdn_w = o.shape[-1]
    tiles_per_mod = rows_per_mod // tm
    y2 = y.reshape(N_DIR, m, ssd_w)
    o2 = o.reshape(N_DIR, m, gdn_w)
    return pl.pallas_call(
        _outproj_kernel,
        name="outproj",
        grid=(m // tm,),
        in_specs=[pl.BlockSpec((1, tm, ssd_w), lambda i: (0, i, 0)),
                  pl.BlockSpec((1, tm, ssd_w), lambda i: (1, i, 0)),
                  pl.BlockSpec((1, tm, gdn_w), lambda i: (0, i, 0)),
                  pl.BlockSpec((1, tm, gdn_w), lambda i: (1, i, 0)),
                  pl.BlockSpec((tm, ssd_w + gdn_w), lambda i: (i, 0)),
                  pl.BlockSpec((tm, d), lambda i: (i, 0)),
                  pl.BlockSpec((1, 1, 1, d), lambda i: (i // tiles_per_mod, 0, 0, 2)),
                  pl.BlockSpec((None, ssd_w + gdn_w, d), lambda i: (layer, 0, 0)),
                  pl.BlockSpec((1, ssd_w), lambda i: (0, 0)),
                  pl.BlockSpec((1, gdn_w), lambda i: (0, 0)),
                  pl.BlockSpec((1, d), lambda i: (0, 0))],
        out_specs=pl.BlockSpec((tm, d), lambda i: (i, 0)),
        out_shape=jax.ShapeDtypeStruct((m, d), F32),
        compiler_params=_params("parallel"),
    )(y2, y2, o2, o2, p2d, h2d, gate, w_all, snw, gnw, postw)


def kernel(x, c, ctx, c_ctx, w_ada, b_ada, pre_norm_w, post_norm_w, w_in, conv_ssd_w, conv_ssd_b,
           conv_gdn_w, ssd_a_log, ssd_dt_bias, ssd_d, ssd_norm_w, gdn_a_log, gdn_dt_bias,
           gdn_norm_w, w_out):
    bsz, seq, d = x.shape
    ctx_len = ctx.shape[1]
    depth = w_in.shape[0]
    d_mix = w_out.shape[1]
    ssd_w = d_mix // 2
    gdn_w = d_mix - ssd_w
    ssd_heads = ssd_w // SSD_HEAD_DIM
    gdn_heads = gdn_w // GDN_HEAD_DIM
    bc_w = 2 * SSD_GROUPS * SSD_STATE
    assert ssd_w == gdn_w and ssd_heads + 2 * gdn_heads <= LANE
    assert seq % CHUNK == 0 and ctx_len % CHUNK == 0 and CHUNK % GRID_W == 0

    c_z, c_xs = 0, d_mix
    c_qkv = c_xs + ssd_w
    c_bc = c_qkv + 3 * gdn_w
    c_sm = c_bc + bc_w
    tn = 1024
    wp = -(-(c_sm + N_DIR * LANE) // tn) * tn
    assert c_qkv % gdn_w == 0 and c_xs % ssd_w == 0 and c_bc % bc_w == 0 and c_sm % LANE == 0
    assert c_xs % tn == 0 and c_sm // tn == wp // tn - 1
    beta_lane, g_lane = ssd_heads, ssd_heads + gdn_heads

    o_xbc = d_mix
    o_dt = o_xbc + ssd_w + bc_w
    o_qkv = o_dt + N_DIR * ssd_heads
    o_beta = o_qkv + 3 * gdn_w
    o_alpha = o_beta + N_DIR * gdn_heads

    assert w_in.shape[-1] == o_alpha + N_DIR * gdn_heads
    w_cat = _pack_weights(w_in, wp, d_mix=d_mix, ssd_w=ssd_w, bc_w=bc_w, qkv_w=3 * gdn_w,
                          ssd_heads=ssd_heads, gdn_heads=gdn_heads)

    def conv_layout(ssd_part, gdn_part):
        lead = ssd_part.shape[:-1]
        return jnp.concatenate(
            [jnp.zeros(lead + (d_mix,), F32), ssd_part[..., :ssd_w], gdn_part, ssd_part[..., ssd_w:],
             jnp.zeros(lead + (wp - c_sm,), F32)], axis=-1)

    cw_all = conv_layout(conv_ssd_w.astype(F32), conv_gdn_w.astype(F32))
    cw_all = jnp.concatenate([cw_all, jnp.zeros((depth, 8 - CONV_K, wp), F32)], axis=1)
    cb_all = conv_layout(conv_ssd_b.astype(F32)[:, None], jnp.zeros((depth, 1, 3 * gdn_w), F32))

    def lane_row(vals, lane0):
        return jnp.pad(vals, ((0, 0), (0, 0), (lane0, LANE - lane0 - vals.shape[-1])))

    zeros_rows = jnp.zeros((depth, N_DIR, 6, LANE), F32)
    ssd_par = jnp.concatenate([lane_row(ssd_dt_bias.astype(F32), 0)[:, :, None],
                               lane_row(-jnp.exp(ssd_a_log.astype(F32)), 0)[:, :, None], zeros_rows], axis=2)
    gdn_par = jnp.concatenate([lane_row(gdn_dt_bias.astype(F32), g_lane)[:, :, None],
                               lane_row(-jnp.exp(gdn_a_log.astype(F32)), g_lane)[:, :, None], zeros_rows], axis=2)
    dsk_all = jnp.repeat(ssd_d.astype(F32), SSD_HEAD_DIM, axis=-1)[:, None]
    gnw_all = jnp.tile(gdn_norm_w.astype(F32), (1, gdn_heads))[:, None]
    w_out16 = w_out.astype(BF16)
    head_of_lane = jnp.arange(LANE)[:, None]
    e_q = (jnp.arange(ssd_heads * CHUNK)[None] // CHUNK == head_of_lane).astype(BF16)

    rows = 16
    c_all = jnp.concatenate([c, c_ctx[None], jnp.zeros((rows - bsz - 1, d), c.dtype)], axis=0)
    mod = _modulation(c_all, w_ada, b_ada).reshape(depth, rows, 1, 3 * d)

    h_lat = x.reshape(bsz * seq, d)
    h_ctx = ctx.reshape(bsz * ctx_len, d)
    tm_lat = min(1024, seq)
    tm_ctx = min(1024, bsz * ctx_len)
    tm_out = 512
    conv_cols = (c_xs, c_sm)
    scan_cols = (c_xs, c_qkv, c_bc, c_sm, ssd_w, gdn_w, bc_w)
    ssd_state0 = jnp.zeros((bsz, N_DIR, SSD_GROUPS, SSD_STATE, ssd_w // SSD_GROUPS), F32)
    gdn_state0 = jnp.zeros((bsz, N_DIR, gdn_heads, GDN_HEAD_DIM, GDN_HEAD_DIM), F32)

    for l in range(depth):
        mod_lat = mod[l, :bsz, None]
        mod_ctx = mod[l, bsz:bsz + 1, None]
        pre_w = pre_norm_w[l][None]
        p_lat = _inproj(h_lat, mod_lat, mod_lat, pre_w, w_cat, l, cw_all[l], cb_all[l],
                        rows_per_mod=seq, group=GRID_W, tm=tm_lat, tn=tn, conv_cols=conv_cols)
        p_ctx = _inproj(h_ctx, mod_ctx, mod_ctx, pre_w, w_cat, l, cw_all[l], cb_all[l],
                        rows_per_mod=bsz * ctx_len, group=ctx_len, tm=tm_ctx, tn=tn, conv_cols=conv_cols)
        p_lat3 = p_lat.reshape(bsz, seq, wp)
        p_ctx3 = p_ctx.reshape(bsz, ctx_len, wp)

        y_ctx, ssd_state, o_ctx, gdn_state = _scans(
            p_ctx3, ssd_par[l], gdn_par[l], dsk_all[l], e_q, ssd_state0, gdn_state0,
            cols=scan_cols, beta_lane=beta_lane, g_lane=g_lane)
        y_lat, _, o_lat, _ = _scans(
            p_lat3, ssd_par[l], gdn_par[l], dsk_all[l], e_q, ssd_state, gdn_state,
            cols=scan_cols, beta_lane=beta_lane, g_lane=g_lane)

        snw = ssd_norm_w[l][None].astype(F32)
        post_w = post_norm_w[l][None]
        h_lat_new = _outproj(y_lat, o_lat, p_lat, h_lat, mod_lat, w_out16, l, snw, gnw_all[l], post_w,
                             rows_per_mod=seq, tm=min(tm_out, seq))
        if l < depth - 1:
            h_ctx = _outproj(y_ctx, o_ctx, p_ctx, h_ctx, mod_ctx, w_out16, l, snw, gnw_all[l], post_w,
                             rows_per_mod=bsz * ctx_len, tm=min(tm_out, bsz * ctx_len))
        h_lat = h_lat_new
    return h_lat.reshape(bsz, seq, d)
```

```python
import functools

import jax
import jax.numpy as jnp
from jax import lax
from jax.experimental import pallas as pl
from jax.experimental.pallas import tpu as pltpu

F32 = jnp.float32
BF16 = jnp.bfloat16
HIGHEST = lax.Precision.HIGHEST

EPS = 1e-6
CHUNK = 128
GRID_W = 64
CONV_K = 5
N_DIR = 2
SSD_HEAD_DIM = 64
SSD_STATE = 128
SSD_GROUPS = 2
GDN_HEAD_DIM = 128
LANE = 128
SUBLANE = 8
NORM_UNROLL = 16
SCAN_OUT_DTYPE = jnp.bfloat16
CHUNKS_PER_STEP = 4
GDN_SOLVE_BLOCK = 32
SUB_ROWS = 512
NORM_ROWS = 16
VMEM_LIMIT = 52 * 1024 * 1024

NT = (((1,), (1,)), ((), ()))
TN = (((0,), (0,)), ((), ()))


def _dot(a, b):
    return jnp.dot(a, b, preferred_element_type=F32)


def _dot_nt(a, b):
    return lax.dot_general(a, b, NT, preferred_element_type=F32)


def _dot_tn(a, b):
    return lax.dot_general(a, b, TN, preferred_element_type=F32)


def _sigmoid(x):
    return 1.0 / (1.0 + jnp.exp(-x))


def _silu(x):
    half = 0.5 * x
    return half + half * jnp.tanh(half)


def _softplus(x):
    return jnp.maximum(x, 0.0) + jnp.log(1.0 + jnp.exp(-jnp.abs(x)))


def _params(*sem, flags=None):
    return pltpu.CompilerParams(dimension_semantics=sem, vmem_limit_bytes=VMEM_LIMIT, flags=flags)


def _mod_kernel(c_ref, w_ref, b_ref, o_ref):
    a = _silu(c_ref[...])
    o_ref[0] = jnp.dot(a, w_ref[0], precision=HIGHEST, preferred_element_type=F32) + b_ref[0]


def _modulation(c_all, w_ada, b_ada):
    depth, d, n = w_ada.shape
    rows = c_all.shape[0]
    tn = 768
    return pl.pallas_call(
        _mod_kernel,
        name="adaln_mod",
        grid=(depth, n // tn),
        in_specs=[pl.BlockSpec((rows, d), lambda l, j: (0, 0)),
                  pl.BlockSpec((1, d, tn), lambda l, j: (l, 0, j)),
                  pl.BlockSpec((1, 1, tn), lambda l, j: (l, 0, j))],
        out_specs=pl.BlockSpec((1, rows, tn), lambda l, j: (l, 0, j)),
        out_shape=jax.ShapeDtypeStruct((depth, rows, n), F32),
        compiler_params=_params("parallel", "parallel"),
    )(c_all, w_ada, b_ada.reshape(depth, 1, n))


def _pack_kernel(w_ref, o_ref, *, d_mix, ssd_w, bc_w, qkv_w, ssd_heads, gdn_heads):
    half = d_mix + ssd_w + bc_w
    a = w_ref[0, :, :half]
    b = w_ref[0, :, half:]
    rows = a.shape[0]
    n_dt, n_g = N_DIR * ssd_heads, N_DIR * gdn_heads
    c_qkv = d_mix + ssd_w
    c_bc = c_qkv + qkv_w
    c_sm = c_bc + bc_w
    o_ref[0, :, 0:c_qkv] = a[:, 0:c_qkv].astype(BF16)
    o_ref[0, :, c_qkv:c_bc] = b[:, n_dt:n_dt + qkv_w].astype(BF16)
    o_ref[0, :, c_bc:c_sm] = a[:, c_qkv:c_qkv + bc_w].astype(BF16)
    for dd in range(N_DIR):
        beta0 = n_dt + qkv_w + dd * gdn_heads
        alpha0 = beta0 + n_g
        small = jnp.concatenate(
            [b[:, dd * ssd_heads:(dd + 1) * ssd_heads], b[:, beta0:beta0 + gdn_heads],
             b[:, alpha0:alpha0 + gdn_heads], jnp.zeros((rows, LANE - ssd_heads - 2 * gdn_heads), F32)], axis=1)
        o_ref[0, :, c_sm + dd * LANE:c_sm + (dd + 1) * LANE] = small.astype(BF16)
    pad0 = c_sm + N_DIR * LANE
    o_ref[0, :, pad0:] = jnp.zeros((rows, o_ref.shape[-1] - pad0), BF16)


def _pack_weights(w_in, wp, **dims):
    depth, d, in_dim = w_in.shape
    assert (dims["d_mix"] + dims["ssd_w"] + dims["bc_w"]) % LANE == 0
    rows = 256
    return pl.pallas_call(
        functools.partial(_pack_kernel, **dims),
        name="pack_w_in",
        grid=(depth, d // rows),
        in_specs=[pl.BlockSpec((1, rows, in_dim), lambda l, r: (l, r, 0))],
        out_specs=pl.BlockSpec((1, rows, wp), lambda l, r: (l, r, 0)),
        out_shape=jax.ShapeDtypeStruct((depth, d, wp), BF16),
        compiler_params=_params("parallel", "parallel"),
    )(w_in)


def _inproj_kernel(h_ref, shift_ref, scale_ref, prew_ref, w_ref, cw_ref, cb_ref, o_ref, u_ref,
                   *, group, j_lo, j_hi, mixed_lanes, mixed_used):
    j = pl.program_id(1)
    tm, tn = o_ref.shape
    sub = max(group, min(tm, SUB_ROWS))

    @pl.when(j == 0)
    def _():
        gain = prew_ref[...] * (1.0 + scale_ref[0, 0])
        shift = shift_ref[0, 0]

        def norm_strip(s, carry):
            r0 = pl.multiple_of(s * NORM_ROWS, NORM_ROWS)
            x = h_ref[pl.ds(r0, NORM_ROWS), :]
            y = x * lax.rsqrt(jnp.mean(x * x, axis=-1, keepdims=True) + EPS)
            u_ref[pl.ds(r0, NORM_ROWS), :] = (y * gain + shift).astype(BF16)
            return carry

        lax.fori_loop(0, tm // NORM_ROWS, norm_strip, 0, unroll=NORM_UNROLL)

    def conv_tile(lanes, used):
        cw = cw_ref[:, :lanes]
        bias = cb_ref[:, :lanes]
        row = lax.broadcasted_iota(jnp.int32, (SUBLANE, lanes), 0)
        offs = [k - CONV_K // 2 for k in range(CONV_K)]
        w_mid = [cw[k:k + 1] for k in range(CONV_K)]
        w_first = [jnp.where(row >= -off, w, 0.0) if off < 0 else w for off, w in zip(offs, w_mid)]
        w_last = [jnp.where(row < SUBLANE - off, w, 0.0) if off > 0 else w for off, w in zip(offs, w_mid)]
        last0 = group - SUBLANE

        def taps(pieces, weights, lo, hi):
            out = bias
            for x, w in zip(pieces, weights):
                out = out + x[lo:hi] * w
            return _silu(out)

        n_sub = tm // sub
        acc_next = _dot(u_ref[0:sub, :], w_ref[:, :used])
        for sb in range(n_sub):
            acc = acc_next
            if sb + 1 < n_sub:
                acc_next = _dot(u_ref[(sb + 1) * sub:(sb + 2) * sub, :], w_ref[:, :used])
            if lanes < used:
                o_ref[sb * sub:(sb + 1) * sub, lanes:used] = acc[:, lanes:]
            a = acc[:, :lanes]
            shifted = [a if off == 0 else pltpu.roll(a, (-off) % sub, axis=0) for off in offs]
            for gi in range(sub // group):
                g0 = gi * group
                o0 = sb * sub + g0
                o_ref[o0:o0 + SUBLANE, :lanes] = taps(shifted, w_first, g0, g0 + SUBLANE)
                o_ref[o0 + SUBLANE:o0 + last0, :lanes] = taps(shifted, w_mid, g0 + SUBLANE, g0 + last0)
                o_ref[o0 + last0:o0 + group, :lanes] = taps(shifted, w_last, g0 + last0, g0 + group)
        if used < tn:
            o_ref[:, used:] = jnp.zeros((tm, tn - used), o_ref.dtype)

    @pl.when(j < j_lo)
    def _():
        o_ref[...] = _silu(_dot(u_ref[...], w_ref[...]))

    @pl.when((j >= j_lo) & (j < j_hi))
    def _():
        conv_tile(tn, tn)

    @pl.when(j >= j_hi)
    def _():
        conv_tile(mixed_lanes, mixed_used)


def _inproj(h2d, shift, scale, pre_w, w_all, layer, cw, cb, *, rows_per_mod, group, tm, tn, conv_cols):
    m, d = h2d.shape
    wp = w_all.shape[-1]
    tiles_per_mod = rows_per_mod // tm
    kern = functools.partial(_inproj_kernel, group=group, j_lo=conv_cols[0] // tn, j_hi=conv_cols[1] // tn,
                             mixed_lanes=conv_cols[1] % tn, mixed_used=(conv_cols[2] - 1) % tn + 1)
    return pl.pallas_call(
        kern,
        name="inproj",
        grid=(m // tm, wp // tn),
        in_specs=[pl.BlockSpec((tm, d), lambda i, j: (i, 0)),
                  pl.BlockSpec((1, 1, 1, d), lambda i, j: (i // tiles_per_mod, 0, 0, 0)),
                  pl.BlockSpec((1, 1, 1, d), lambda i, j: (i // tiles_per_mod, 0, 0, 1)),
                  pl.BlockSpec((1, d), lambda i, j: (0, 0)),
                  pl.BlockSpec((None, d, tn), lambda i, j: (layer, 0, j)),
                  pl.BlockSpec((8, tn), lambda i, j: (0, j)),
                  pl.BlockSpec((1, tn), lambda i, j: (0, j))],
        out_specs=pl.BlockSpec((tm, tn), lambda i, j: (i, j)),
        out_shape=jax.ShapeDtypeStruct((m, wp), F32),
        scratch_shapes=[pltpu.VMEM((tm, d), BF16)],
        compiler_params=_params("parallel", "arbitrary"),
    )(h2d, shift, scale, pre_w, w_all, cw, cb)


def _chunk_index(d, p, nc):
    return p + d * (nc - 1 - 2 * p)


def _causal_masks(backward):
    ri = lax.broadcasted_iota(jnp.int32, (CHUNK, CHUNK), 0)
    ci = lax.broadcasted_iota(jnp.int32, (CHUNK, CHUNK), 1)
    return ((ri <= ci), (ri < ci)) if backward else ((ri >= ci), (ri > ci))


def _log_decay_cumsums(sm_ref, spar_ref, gpar_ref, rows, backward):
    tri = _causal_masks(backward)[0].astype(BF16)
    spar = spar_ref[0]
    gpar = gpar_ref[0]
    out = []
    for r in rows:
        sm = sm_ref[0, r:r + CHUNK, :]
        x = _softplus(sm + spar[0:1]) * spar[1:2] + _softplus(sm + gpar[0:1]) * gpar[1:2]
        hi = x.astype(BF16)
        rest = x - hi.astype(F32)
        mid = rest.astype(BF16)
        lo = (rest - mid.astype(F32)).astype(BF16)
        parts = _dot(tri, jnp.concatenate([hi, mid, lo], axis=1))
        out.append(parts[:, :LANE] + parts[:, LANE:2 * LANE] + parts[:, 2 * LANE:])
    return out


def _ssd_chunks(xs_ref, bc_ref, sm_ref, par_ref, dsk_ref, eq_ref, y_ref, hT_ref, cums, backward):
    heads_per_group = xs_ref.shape[-1] // (SSD_GROUPS * SSD_HEAD_DIM)
    gw = heads_per_group * SSD_HEAD_DIM
    incl = _causal_masks(backward)[0]
    last = 0 if backward else CHUNK - 1
    par = par_ref[0]
    cps = xs_ref.shape[1] // CHUNK
    rows = [((cps - 1 - s) if backward else s) * CHUNK for s in range(cps)]
    groups = range(SSD_GROUPS)
    ppg = gw // LANE
    pairs = range(SSD_GROUPS * ppg)
    left = lax.broadcasted_iota(jnp.int32, (CHUNK, LANE), 1) < SSD_HEAD_DIM

    cg, xp, yd, eacum_x, cdec, upd = [], [], [], [], [], []
    for s in range(cps):
        sm = sm_ref[0, rows[s]:rows[s] + CHUNK, :]
        xs = xs_ref[0, rows[s]:rows[s] + CHUNK, :]
        bc = bc_ref[0, rows[s]:rows[s] + CHUNK, :]
        dt = _softplus(sm + par[0:1])
        acum = cums[s]
        dt_t = dt.T
        hi = acum.astype(BF16)
        lo = (acum - hi.astype(F32)).astype(BF16)
        acum_q = _dot(hi, eq_ref[...]) + _dot(lo, eq_ref[...])
        acum_t = (hi.astype(F32) + lo.astype(F32)).T
        bg = [bc[:, g * SSD_STATE:(g + 1) * SSD_STATE].astype(BF16) for g in groups]
        cgs = [bc[:, (SSD_GROUPS + g) * SSD_STATE:(SSD_GROUPS + g + 1) * SSD_STATE].astype(BF16) for g in groups]
        cb = [_dot_nt(cgs[g], bg[g]) for g in groups]

        def intra(hh):
            seg = acum_q[:, hh * CHUNK:(hh + 1) * CHUNK] - acum_t[hh:hh + 1, :]
            return (cb[hh // heads_per_group] * jnp.where(incl, jnp.exp(seg), 0.0)
                    * dt_t[hh:hh + 1, :]).astype(BF16)

        xps = [xs[:, i * LANE:(i + 1) * LANE] for i in pairs]
        m_pair = [jnp.concatenate([intra(2 * i), intra(2 * i + 1)], axis=1) for i in pairs]
        x_bd = [jnp.concatenate([jnp.where(left, x, 0.0), jnp.where(left, 0.0, x)], axis=0).astype(BF16)
                for x in xps]
        yd.append([_dot(m_pair[i], x_bd[i]) for i in pairs])
        acum_x = [jnp.where(left, acum_q[:, 2 * i * CHUNK:2 * i * CHUNK + LANE],
                            acum_q[:, (2 * i + 1) * CHUNK:(2 * i + 1) * CHUNK + LANE]) for i in pairs]
        alast_x = [a[last:last + 1] for a in acum_x]
        dt_x = [jnp.where(left, dt[:, 2 * i:2 * i + 1], dt[:, 2 * i + 1:2 * i + 2]) for i in pairs]
        xw = [(xps[i] * (dt_x[i] * jnp.exp(alast_x[i] - acum_x[i]))).astype(BF16) for i in pairs]
        upd.append([_dot_tn(bg[g], jnp.concatenate(xw[g * ppg:(g + 1) * ppg], axis=1)) for g in groups])
        cdec.append([jnp.exp(jnp.concatenate(alast_x[g * ppg:(g + 1) * ppg], axis=1)) for g in groups])
        eacum_x.append([jnp.exp(a) for a in acum_x])
        cg.append(cgs)
        xp.append(xps)

    h_t = [hT_ref[0, 0, g] for g in groups]
    for s in range(cps):
        yoff = [_dot(cg[s][g], h_t[g].astype(BF16)) for g in groups]
        h_t = [h_t[g] * cdec[s][g] + upd[s][g] for g in groups]
        for i in pairs:
            lo = i * LANE
            yo = yoff[i // ppg][:, (i % ppg) * LANE:(i % ppg + 1) * LANE]
            y = yd[s][i] + yo * eacum_x[s][i]
            if not backward:
                y = y + xp[s][i] * dsk_ref[:, lo:lo + LANE]
            y_ref[0, 0, rows[s]:rows[s] + CHUNK, lo:lo + LANE] = y.astype(y_ref.dtype)
    for g in groups:
        hT_ref[0, 0, g] = h_t[g]


def _diag_block_inverses(a_list, xor_ij, block):
    eye = jnp.where(xor_ij == 0, 1.0, 0.0).astype(BF16)
    in_pair = (xor_ij < 2).astype(BF16)
    t_list = [eye - a * in_pair for a in a_list]
    size = 2
    while size < block:
        couples = ((xor_ij >> (size.bit_length() - 1)) == 1).astype(BF16)
        ta = [_dot(t, a * couples).astype(BF16) for t, a in zip(t_list, a_list)]
        t_list = [t - _dot(x, t).astype(BF16) for t, x in zip(t_list, ta)]
        size *= 2
    return t_list


def _block_substitution(a_list, t_list, rhs_list, block, backward):
    n = a_list[0].shape[0]
    width = rhs_list[0].shape[1]
    order = range(n // block)
    order = list(reversed(order)) if backward else list(order)
    a16, t16 = a_list, t_list

    def place(y, i):
        parts = []
        if i > 0:
            parts.append(jnp.zeros((i * block, width), y.dtype))
        parts.append(y)
        if (i + 1) * block < n:
            parts.append(jnp.zeros((n - (i + 1) * block, width), y.dtype))
        return jnp.concatenate(parts, axis=0) if len(parts) > 1 else y

    x_list = [None] * len(a_list)
    for step, i in enumerate(order):
        rows = slice(i * block, (i + 1) * block)
        y = [r[rows] for r in rhs_list]
        if step > 0:
            y = [yy - _dot(a[rows], x.astype(BF16)) for yy, a, x in zip(y, a16, x_list)]
        xi = [_dot(t[rows], place(yy.astype(BF16), i)) for t, yy in zip(t16, y)]
        x_list = [place(v, i) if x is None else x + place(v, i) for x, v in zip(x_list, xi)]
    return x_list


def _gdn_chunks(q_ref, k_ref, v_ref, sm_ref, o_ref, s_ref, cums, is_backward, *, beta_lane, g_lane):
    heads = range(q_ref.shape[-1] // GDN_HEAD_DIM)
    kk = GDN_HEAD_DIM
    cps = q_ref.shape[1] // CHUNK
    units = [(s, h) for s in range(cps) for h in heads]
    ri = lax.broadcasted_iota(jnp.int32, (CHUNK, CHUNK), 0)
    ci = lax.broadcasted_iota(jnp.int32, (CHUNK, CHUNK), 1)
    xor_ij = ri ^ ci

    def col(x, lane):
        return x[:, lane:lane + 1]

    def scan(backward):
        incl, strict = _causal_masks(backward)
        last = 0 if backward else CHUNK - 1
        rows = [((cps - 1 - s) if backward else s) * CHUNK for s in range(cps)]

        def head_cols(ref, s, h):
            return ref[0, rows[s]:rows[s] + CHUNK, h * kk:(h + 1) * kk]

        eg, eend, cdec, beta_all, gcum, gcum_t = [], [], [], [], [], []
        for s in range(cps):
            sm = sm_ref[0, rows[s]:rows[s] + CHUNK, :]
            beta_all.append(_sigmoid(sm))
            gc = cums[s]
            glast = gc[last:last + 1]
            gcum.append(gc)
            gcum_t.append(gc.T)
            eg.append(jnp.exp(gc))
            eend.append(jnp.exp(glast - gc))
            cdec.append(jnp.exp(glast))
        qn, kn = {}, {}
        for s, h in units:
            q = head_cols(q_ref, s, h)
            k = head_cols(k_ref, s, h)
            qn[s, h] = q * (lax.rsqrt(jnp.sum(q * q, axis=-1, keepdims=True) + EPS) * kk ** -0.5)
            kn[s, h] = k * lax.rsqrt(jnp.sum(k * k, axis=-1, keepdims=True) + EPS)
        beta = {(s, h): col(beta_all[s], beta_lane + h) for s, h in units}
        dec = {(s, h): jnp.where(
            incl, jnp.exp(col(gcum[s], g_lane + h) - gcum_t[s][g_lane + h:g_lane + h + 1, :]), 0.0)
            for s, h in units}
        kb = {x: kn[x] * beta[x] for x in units}
        k16 = {x: kn[x].astype(BF16) for x in units}
        a = [jnp.where(strict, _dot_nt(kb[x].astype(BF16), k16[x]) * dec[x], 0.0).astype(BF16) for x in units]
        attn = {x: (_dot_nt(qn[x].astype(BF16), k16[x]) * dec[x]).astype(BF16) for x in units}
        rhs = [jnp.concatenate([head_cols(v_ref, s, h) * beta[s, h], kb[s, h] * col(eg[s], g_lane + h)], axis=1)
               for s, h in units]
        t = _diag_block_inverses(a, xor_ij, GDN_SOLVE_BLOCK)
        sol = dict(zip(units, _block_substitution(a, t, rhs, GDN_SOLVE_BLOCK, backward)))
        wq = {(s, h): jnp.concatenate([sol[s, h][:, kk:].astype(BF16),
                                       (qn[s, h] * col(eg[s], g_lane + h)).astype(BF16)], axis=0)
              for s, h in units}
        ke = {(s, h): (kn[s, h] * col(eend[s], g_lane + h)).astype(BF16) for s, h in units}

        state = [s_ref[0, 0, h] for h in heads]
        for s in range(cps):
            ws = [_dot(wq[s, h], state[h].astype(BF16)) for h in heads]
            vnew = [(sol[s, h][:, :kk] - ws[h][:CHUNK]).astype(BF16) for h in heads]
            o = [ws[h][CHUNK:] + _dot(attn[s, h], vnew[h]) for h in heads]
            state = [state[h] * col(cdec[s], g_lane + h) + _dot_tn(ke[s, h], vnew[h]) for h in heads]
            for h in heads:
                o_ref[0, 0, rows[s]:rows[s] + CHUNK, h * kk:(h + 1) * kk] = o[h].astype(o_ref.dtype)
        for h in heads:
            s_ref[0, 0, h] = state[h]

    scan(is_backward)


def _scan_kernel(xs_ref, bc_ref, q_ref, k_ref, v_ref, sm_ref, spar_ref, gpar_ref, dsk_ref, eq_ref,
                 h0_ref, s0_ref, y_ref, hT_ref, o_ref, s_ref, *, beta_lane, g_lane):
    d = pl.program_id(1)
    p = pl.program_id(2)

    @pl.when(p == 0)
    def _():
        hT_ref[...] = h0_ref[...]
        s_ref[...] = s0_ref[...]

    def scans(backward):
        cps = sm_ref.shape[1] // CHUNK
        rows = [((cps - 1 - s) if backward else s) * CHUNK for s in range(cps)]
        cums = _log_decay_cumsums(sm_ref, spar_ref, gpar_ref, rows, backward)
        _gdn_chunks(q_ref, k_ref, v_ref, sm_ref, o_ref, s_ref, cums, backward,
                    beta_lane=beta_lane, g_lane=g_lane)
        _ssd_chunks(xs_ref, bc_ref, sm_ref, spar_ref, dsk_ref, eq_ref, y_ref, hT_ref, cums, backward)

    @pl.when(d == 0)
    def _():
        scans(False)

    @pl.when(d == 1)
    def _():
        scans(True)


def _scans(p3d, ssd_par, gdn_par, dsk, e_q, h0, s0, *, cols, beta_lane, g_lane):
    bsz, length, _ = p3d.shape
    nc = length // CHUNK
    c_xs, c_qkv, c_bc, c_sm, w_xs, w_q, w_bc = cols
    cps = min(CHUNKS_PER_STEP, nc)
    nb = nc // cps
    rows = cps * CHUNK
    chunk = functools.partial(_chunk_index, nc=nb)

    def rows_spec(width, col_block):
        return pl.BlockSpec((1, rows, width), lambda b, d, p: (b, chunk(d, p), col_block))

    def state_spec(shape):
        return pl.BlockSpec((1, 1) + shape[2:], lambda b, d, p: (b, d) + (0,) * (len(shape) - 2))

    def out_spec(width):
        return pl.BlockSpec((1, 1, rows, width), lambda b, d, p: (d, b, chunk(d, p), 0))

    return pl.pallas_call(
        functools.partial(_scan_kernel, beta_lane=beta_lane, g_lane=g_lane),
        name="scans",
        grid=(bsz, N_DIR, nb),
        in_specs=[rows_spec(w_xs, c_xs // w_xs), rows_spec(w_bc, c_bc // w_bc)]
        + [rows_spec(w_q, c_qkv // w_q + i) for i in range(3)]
        + [pl.BlockSpec((1, rows, LANE), lambda b, d, p: (b, chunk(d, p), c_sm // LANE + d)),
           pl.BlockSpec((1, 8, LANE), lambda b, d, p: (d, 0, 0)),
           pl.BlockSpec((1, 8, LANE), lambda b, d, p: (d, 0, 0)),
           pl.BlockSpec((1, w_xs), lambda b, d, p: (0, 0)),
           pl.BlockSpec(e_q.shape, lambda b, d, p: (0, 0)),
           state_spec(h0.shape), state_spec(s0.shape)],
        out_specs=[out_spec(w_xs), state_spec(h0.shape), out_spec(w_q), state_spec(s0.shape)],
        out_shape=[jax.ShapeDtypeStruct((N_DIR, bsz, length, w_xs), SCAN_OUT_DTYPE),
                   jax.ShapeDtypeStruct(h0.shape, F32),
                   jax.ShapeDtypeStruct((N_DIR, bsz, length, w_q), SCAN_OUT_DTYPE),
                   jax.ShapeDtypeStruct(s0.shape, F32)],
        compiler_params=_params("parallel", "parallel", "arbitrary"),
    )(p3d, p3d, p3d, p3d, p3d, p3d, ssd_par, gdn_par, dsk, e_q, h0, s0)


def _outproj_kernel(y0_ref, y1_ref, o0_ref, o1_ref, z_ref, h_ref, gate_ref, w_ref,
                    snw_ref, gnw_ref, postw_ref, out_ref):
    zs = z_ref[...]
    ssd_w = y0_ref.shape[-1]
    gw = ssd_w // SSD_GROUPS
    ys = (y0_ref[0].astype(F32) + y1_ref[0].astype(F32)) * zs[:, :ssd_w]
    parts = []
    for g in range(SSD_GROUPS):
        yg = ys[:, g * gw:(g + 1) * gw]
        parts.append(yg * lax.rsqrt(jnp.mean(yg * yg, axis=-1, keepdims=True) + EPS))
    ysn = jnp.concatenate(parts, axis=1) * snw_ref[...]
    og = o0_ref[0].astype(F32) + o1_ref[0].astype(F32)
    parts = []
    for h in range(og.shape[-1] // GDN_HEAD_DIM):
        oh = og[:, h * GDN_HEAD_DIM:(h + 1) * GDN_HEAD_DIM]
        parts.append(oh * lax.rsqrt(jnp.mean(oh * oh, axis=-1, keepdims=True) + EPS))
    ogn = jnp.concatenate(parts, axis=1) * gnw_ref[...] * zs[:, ssd_w:]
    ycat = jnp.concatenate([ysn, ogn], axis=1).astype(BF16)
    m = _dot(ycat, w_ref[...])
    mn = m * lax.rsqrt(jnp.mean(m * m, axis=-1, keepdims=True) + EPS) * postw_ref[...]
    out_ref[...] = h_ref[...] + gate_ref[0, 0] * mn


def _outproj(y, o, p2d, h2d, gate, w_all, layer, snw, gnw, postw, *, rows_per_mod, tm):
    m, d = h2d.shape
    ssd_w = y.shape[-1]
    gdn_w = o.shape[-1]
    tiles_per_mod = rows_per_mod // tm
    y2 = y.reshape(N_DIR, m, ssd_w)
    o2 = o.reshape(N_DIR, m, gdn_w)
    return pl.pallas_call(
        _outproj_kernel,
        name="outproj",
        grid=(m // tm,),
        in_specs=[pl.BlockSpec((1, tm, ssd_w), lambda i: (0, i, 0)),
                  pl.BlockSpec((1, tm, ssd_w), lambda i: (1, i, 0)),
                  pl.BlockSpec((1, tm, gdn_w), lambda i: (0, i, 0)),
                  pl.BlockSpec((1, tm, gdn_w), lambda i: (1, i, 0)),
                  pl.BlockSpec((tm, ssd_w + gdn_w), lambda i: (i, 0)),
                  pl.BlockSpec((tm, d), lambda i: (i, 0)),
                  pl.BlockSpec((1, 1, 1, d), lambda i: (i // tiles_per_mod, 0, 0, 2)),
                  pl.BlockSpec((None, ssd_w + gdn_w, d), lambda i: (layer, 0, 0)),
                  pl.BlockSpec((1, ssd_w), lambda i: (0, 0)),
                  pl.BlockSpec((1, gdn_w), lambda i: (0, 0)),
                  pl.BlockSpec((1, d), lambda i: (0, 0))],
        out_specs=pl.BlockSpec((tm, d), lambda i: (i, 0)),
        out_shape=jax.ShapeDtypeStruct((m, d), F32),
        compiler_params=_params("parallel"),
    )(y2, y2, o2, o2, p2d, h2d, gate, w_all, snw, gnw, postw)


def kernel(x, c, ctx, c_ctx, w_ada, b_ada, pre_norm_w, post_norm_w, w_in, conv_ssd_w, conv_ssd_b,
           conv_gdn_w, ssd_a_log, ssd_dt_bias, ssd_d, ssd_norm_w, gdn_a_log, gdn_dt_bias,
           gdn_norm_w, w_out):
    bsz, seq, d = x.shape
    ctx_len = ctx.shape[1]
    depth = w_in.shape[0]
    d_mix = w_out.shape[1]
    ssd_w = d_mix // 2
    gdn_w = d_mix - ssd_w
    ssd_heads = ssd_w // SSD_HEAD_DIM
    gdn_heads = gdn_w // GDN_HEAD_DIM
    bc_w = 2 * SSD_GROUPS * SSD_STATE
    assert ssd_w == gdn_w and ssd_heads + 2 * gdn_heads <= LANE
    assert seq % CHUNK == 0 and ctx_len % CHUNK == 0 and CHUNK % GRID_W == 0

    c_z, c_xs = 0, d_mix
    c_qkv = c_xs + ssd_w
    c_bc = c_qkv + 3 * gdn_w
    c_sm = c_bc + bc_w
    tn = 1024
    wp = -(-(c_sm + N_DIR * LANE) // tn) * tn
    assert c_qkv % gdn_w == 0 and c_xs % ssd_w == 0 and c_bc % bc_w == 0 and c_sm % LANE == 0
    assert c_xs % tn == 0 and c_sm // tn == wp // tn - 1
    beta_lane, g_lane = ssd_heads, ssd_heads + gdn_heads

    o_xbc = d_mix
    o_dt = o_xbc + ssd_w + bc_w
    o_qkv = o_dt + N_DIR * ssd_heads
    o_beta = o_qkv + 3 * gdn_w
    o_alpha = o_beta + N_DIR * gdn_heads

    assert w_in.shape[-1] == o_alpha + N_DIR * gdn_heads
    w_cat = _pack_weights(w_in, wp, d_mix=d_mix, ssd_w=ssd_w, bc_w=bc_w, qkv_w=3 * gdn_w,
                          ssd_heads=ssd_heads, gdn_heads=gdn_heads)

    def conv_layout(ssd_part, gdn_part):
        lead = ssd_part.shape[:-1]
        return jnp.concatenate(
            [jnp.zeros(lead + (d_mix,), F32), ssd_part[..., :ssd_w], gdn_part, ssd_part[..., ssd_w:],
             jnp.zeros(lead + (wp - c_sm,), F32)], axis=-1)

    cw_all = conv_layout(conv_ssd_w.astype(F32), conv_gdn_w.astype(F32))
    cw_all = jnp.concatenate([cw_all, jnp.zeros((depth, 8 - CONV_K, wp), F32)], axis=1)
    cb_all = conv_layout(conv_ssd_b.astype(F32)[:, None], jnp.zeros((depth, 1, 3 * gdn_w), F32))

    def lane_row(vals, lane0):
        return jnp.pad(vals, ((0, 0), (0, 0), (lane0, LANE - lane0 - vals.shape[-1])))

    zeros_rows = jnp.zeros((depth, N_DIR, 6, LANE), F32)
    ssd_par = jnp.concatenate([lane_row(ssd_dt_bias.astype(F32), 0)[:, :, None],
                               lane_row(-jnp.exp(ssd_a_log.astype(F32)), 0)[:, :, None], zeros_rows], axis=2)
    gdn_par = jnp.concatenate([lane_row(gdn_dt_bias.astype(F32), g_lane)[:, :, None],
                               lane_row(-jnp.exp(gdn_a_log.astype(F32)), g_lane)[:, :, None], zeros_rows], axis=2)
    dsk_all = jnp.repeat(ssd_d.astype(F32), SSD_HEAD_DIM, axis=-1)[:, None]
    gnw_all = jnp.tile(gdn_norm_w.astype(F32), (1, gdn_heads))[:, None]
    w_out16 = w_out.astype(BF16)
    head_of_lane = jnp.arange(LANE)[:, None]
    e_q = (jnp.arange(ssd_heads * CHUNK)[None] // CHUNK == head_of_lane).astype(BF16)

    rows = 16
    c_all = jnp.concatenate([c, c_ctx[None], jnp.zeros((rows - bsz - 1, d), c.dtype)], axis=0)
    mod = _modulation(c_all, w_ada, b_ada).reshape(depth, rows, 1, 3 * d)

    h_lat = x.reshape(bsz * seq, d)
    h_ctx = ctx.reshape(bsz * ctx_len, d)
    tm_lat = min(1024, seq)
    tm_ctx = min(1024, bsz * ctx_len)
    tm_out = 512
    conv_cols = (c_xs, c_sm, c_sm + N_DIR * LANE)
    scan_cols = (c_xs, c_qkv, c_bc, c_sm, ssd_w, gdn_w, bc_w)
    ssd_state0 = jnp.zeros((bsz, N_DIR, SSD_GROUPS, SSD_STATE, ssd_w // SSD_GROUPS), F32)
    gdn_state0 = jnp.zeros((bsz, N_DIR, gdn_heads, GDN_HEAD_DIM, GDN_HEAD_DIM), F32)

    for l in range(depth):
        mod_lat = mod[l, :bsz, None]
        mod_ctx = mod[l, bsz:bsz + 1, None]
        pre_w = pre_norm_w[l][None]
        p_lat = _inproj(h_lat, mod_lat, mod_lat, pre_w, w_cat, l, cw_all[l], cb_all[l],
                        rows_per_mod=seq, group=GRID_W, tm=tm_lat, tn=tn, conv_cols=conv_cols)
        p_ctx = _inproj(h_ctx, mod_ctx, mod_ctx, pre_w, w_cat, l, cw_all[l], cb_all[l],
                        rows_per_mod=bsz * ctx_len, group=ctx_len, tm=tm_ctx, tn=tn, conv_cols=conv_cols)
        p_lat3 = p_lat.reshape(bsz, seq, wp)
        p_ctx3 = p_ctx.reshape(bsz, ctx_len, wp)

        y_ctx, ssd_state, o_ctx, gdn_state = _scans(
            p_ctx3, ssd_par[l], gdn_par[l], dsk_all[l], e_q, ssd_state0, gdn_state0,
            cols=scan_cols, beta_lane=beta_lane, g_lane=g_lane)
        y_lat, _, o_lat, _ = _scans(
            p_lat3, ssd_par[l], gdn_par[l], dsk_all[l], e_q, ssd_state, gdn_state,
            cols=scan_cols, beta_lane=beta_lane, g_lane=g_lane)

        snw = ssd_norm_w[l][None].astype(F32)
        post_w = post_norm_w[l][None]
        h_lat_new = _outproj(y_lat, o_lat, p_lat, h_lat, mod_lat, w_out16, l, snw, gnw_all[l], post_w,
                             rows_per_mod=seq, tm=min(tm_out, seq))
        if l < depth - 1:
            h_ctx = _outproj(y_ctx, o_ctx, p_ctx, h_ctx, mod_ctx, w_out16, l, snw, gnw_all[l], post_w,
                             rows_per_mod=bsz * ctx_len, tm=min(tm_out, bsz * ctx_len))
        h_lat = h_lat_new
    return h_lat.reshape(bsz, seq, d)
```
